```python
import math, functools
import jax, jax.numpy as jnp
from jax import lax
import numpy as np

D_MODEL = 1024
BATCH = 8
SEQ = 4096
DEPTH = 2
DEC_BATCH = 32
DEC_SEQ = 8
PAST_LEN = 16384
PAGE_SIZE = 128

MIX_WIDTH = D_MODEL
ATT_WIDTH = MIX_WIDTH // 2
POOL_WIDTH = MIX_WIDTH - ATT_WIDTH
HEAD_DIM = 64
N_ATT_HEADS = ATT_WIDTH // HEAD_DIM
POOL_WINDOWS = (2, 4, 8, 16)
N_POOL_GROUPS = len(POOL_WINDOWS)
POOL_GROUP = POOL_WIDTH // N_POOL_GROUPS
POOL_BUF = max(POOL_WINDOWS) - 1
IN_WIDTH = 3 * ATT_WIDTH + POOL_WIDTH + N_ATT_HEADS
D_FF = 4 * D_MODEL
PLE_DIM = 256
QBLOCK = 128
EPS = 1e-6
NEG_INF = -1e30
ATT_SCALE = 1.0 / math.sqrt(HEAD_DIM)

kernel_name = "hymba_fox_pool_sandwich_decoder_step"

F32 = jnp.float32


def rms_norm(x, g):
    x32 = x.astype(F32)
    y = x32 * lax.rsqrt(jnp.mean(x32 * x32, axis=-1, keepdims=True) + EPS)
    return (y * g.astype(F32)).astype(x.dtype)


def split_projection(a, w_in, b_f):
    z = a @ w_in
    lead = z.shape[:-1]
    A = ATT_WIDTH
    q = z[..., :A].reshape(*lead, N_ATT_HEADS, HEAD_DIM)
    k = z[..., A:2 * A].reshape(*lead, N_ATT_HEADS, HEAD_DIM)
    v = z[..., 2 * A:3 * A].reshape(*lead, N_ATT_HEADS, HEAD_DIM)
    u = z[..., 3 * A:3 * A + POOL_WIDTH]
    f_logit = z[..., 3 * A + POOL_WIDTH:].astype(F32) + b_f.astype(F32)
    lf = jax.nn.log_sigmoid(f_logit)
    return q, k, v, u, lf


def fox_prompt_attend(q, k, v, lf):
    B, S, H, Dh = q.shape
    nb = S // QBLOCK
    c = jnp.cumsum(lf, axis=1)
    c_k = jnp.transpose(c, (0, 2, 1))
    k32 = k.astype(F32)
    qb = jnp.moveaxis(q.reshape(B, nb, QBLOCK, H, Dh), 1, 0)
    cb = jnp.moveaxis(c.reshape(B, nb, QBLOCK, H), 1, 0)
    key_pos = jnp.arange(S)

    def block(args):
        qi, ci, bi = args
        s = jnp.einsum('bqhd,bkhd->bhqk', qi.astype(F32), k32) * ATT_SCALE
        s = s + jnp.transpose(ci, (0, 2, 1))[..., None] - c_k[:, :, None, :]
        qpos = bi * QBLOCK + jnp.arange(QBLOCK)
        s = jnp.where(key_pos[None, :] <= qpos[:, None], s, NEG_INF)
        p = jax.nn.softmax(s, axis=-1)
        return jnp.einsum('bhqk,bkhd->bqhd', p.astype(v.dtype), v)

    o = lax.map(block, (qb, cb, jnp.arange(nb)))
    return jnp.moveaxis(o, 0, 1).reshape(B, S, H * Dh)


def fox_sample_attend(q, k, v, lf, k_past, v_past, lf_past):
    B, T, H, Dh = q.shape
    P = k_past.shape[1]
    k_all = jnp.concatenate([k_past.astype(k.dtype), k], axis=1)
    v_all = jnp.concatenate([v_past.astype(v.dtype), v], axis=1)
    lf_all = jnp.concatenate([lf_past.astype(F32), lf.astype(F32)], axis=1)
    c = jnp.cumsum(lf_all, axis=1)
    c_k = jnp.transpose(c, (0, 2, 1))
    c_q = c_k[:, :, P:]
    s = jnp.einsum('bqhd,bkhd->bhqk', q.astype(F32), k_all.astype(F32)) * ATT_SCALE
    s = s + c_q[..., None] - c_k[:, :, None, :]
    mask = jnp.arange(P + T)[None, :] <= (P + jnp.arange(T))[:, None]
    s = jnp.where(mask, s, NEG_INF)
    p = jax.nn.softmax(s, axis=-1)
    o = jnp.einsum('bhqk,bkhd->bqhd', p.astype(v_all.dtype), v_all)
    return o.reshape(B, T, H * Dh)


def pool_mix(u_hist, u, pos0, w_pool, pool_scale):
    B, T, _ = u.shape
    L = u_hist.shape[1]
    z = jnp.concatenate([u_hist.astype(u.dtype), u], axis=1)
    cs = jnp.cumsum(z.astype(F32), axis=1)
    cs = jnp.pad(cs, ((0, 0), (1, 0), (0, 0)))
    end = cs[:, L + 1:L + 1 + T]
    pos_count = pos0 + jnp.arange(T) + 1
    means = []
    for g, w in enumerate(POOL_WINDOWS):
        sl = slice(g * POOL_GROUP, (g + 1) * POOL_GROUP)
        start = cs[:, L + 1 - w:L + 1 - w + T, sl]
        cnt = jnp.minimum(pos_count, w).astype(F32)[None, :, None]
        means.append((end[..., sl] - start) / cnt)
    pooled = jnp.concatenate(means, axis=-1) - u.astype(F32)
    pooled = pooled.reshape(B, T, N_POOL_GROUPS, POOL_GROUP)
    y = jnp.einsum('btgc,gcd->btgd', pooled, w_pool.astype(F32)).reshape(B, T, POOL_WIDTH)
    y = y * pool_scale.astype(F32)
    return y.astype(u.dtype), z[:, -L:]


def trunk_layer(h, p_i, attend, pool_hist, pos0, w_in, b_f, w_pool, pool_scale, g_fox, g_pool,
                w_out, g_pre_mix, g_post_mix, g_pre_mlp, g_post_mlp, w_up, w_down,
                w_ple, g_ple, w_ple_gate):
    a = rms_norm(h, g_pre_mix)
    q, k, v, u, lf = split_projection(a, w_in, b_f)
    att = attend(q, k, v, lf)
    pool, pool_state = pool_mix(pool_hist, u, pos0, w_pool, pool_scale)
    mix = jnp.concatenate([rms_norm(att, g_fox), rms_norm(pool, g_pool)], axis=-1) @ w_out
    h = h + rms_norm(mix, g_post_mix)
    m = rms_norm(h, g_pre_mlp)
    m = jnp.square(jax.nn.relu(m @ w_up)) @ w_down
    h = h + rms_norm(m, g_post_mlp)
    e = rms_norm(p_i.astype(h.dtype) @ w_ple, g_ple)
    h = h + jax.nn.sigmoid(h @ w_ple_gate) * e
    return h, k, v, lf, pool_state


def setup_inputs(seed: int = 0) -> dict:
    key = jax.random.key(seed)
    ks = jax.random.split(key, 32)
    n_pages = PAST_LEN // PAGE_SIZE
    n_used = DEC_BATCH * n_pages
    n_pool = (n_used * 5) // 4
    nrm = jax.random.normal
    f_bias = jnp.linspace(1.0, 6.0, N_ATT_HEADS, dtype=F32)

    x_prompt = nrm(ks[0], (BATCH, SEQ, D_MODEL), F32)
    x_sample = nrm(ks[1], (DEC_BATCH, DEC_SEQ, D_MODEL), F32)
    cache_k = nrm(ks[2], (DEPTH, n_pool, PAGE_SIZE, N_ATT_HEADS, HEAD_DIM), F32)
    cache_v = nrm(ks[3], (DEPTH, n_pool, PAGE_SIZE, N_ATT_HEADS, HEAD_DIM), F32)
    cache_logf = jax.nn.log_sigmoid(nrm(ks[4], (DEPTH, n_pool, PAGE_SIZE, N_ATT_HEADS), F32) + f_bias)
    state_pool = nrm(ks[5], (DEPTH, DEC_BATCH, POOL_BUF, POOL_WIDTH), F32)
    perm = jax.random.permutation(ks[6], n_pool)
    page_table = perm[:n_used].reshape(DEC_BATCH, n_pages).astype(jnp.int32)
    p_prompt = nrm(ks[7], (DEPTH, BATCH, SEQ, PLE_DIM), F32)
    p_sample = nrm(ks[8], (DEPTH, DEC_BATCH, DEC_SEQ, PLE_DIM), F32)

    def gain(k, n):
        return 1.0 + 0.05 * nrm(k, (DEPTH, n), F32)

    w_in = nrm(ks[9], (DEPTH, D_MODEL, IN_WIDTH), F32) * D_MODEL ** -0.5
    b_f = f_bias[None, :] + 0.1 * nrm(ks[10], (DEPTH, N_ATT_HEADS), F32)
    w_pool = nrm(ks[11], (DEPTH, N_POOL_GROUPS, POOL_GROUP, POOL_GROUP), F32) * POOL_GROUP ** -0.5
    pool_scale = gain(ks[12], POOL_WIDTH)
    g_fox = gain(ks[13], ATT_WIDTH)
    g_pool = gain(ks[14], POOL_WIDTH)
    w_out = nrm(ks[15], (DEPTH, MIX_WIDTH, D_MODEL), F32) * MIX_WIDTH ** -0.5
    g_pre_mix = gain(ks[16], D_MODEL)
    g_post_mix = gain(ks[17], D_MODEL)
    g_pre_mlp = gain(ks[18], D_MODEL)
    g_post_mlp = gain(ks[19], D_MODEL)
    w_up = nrm(ks[20], (DEPTH, D_MODEL, D_FF), F32) * D_MODEL ** -0.5
    w_down = nrm(ks[21], (DEPTH, D_FF, D_MODEL), F32) * D_FF ** -0.5
    w_ple = nrm(ks[22], (DEPTH, PLE_DIM, D_MODEL), F32) * PLE_DIM ** -0.5
    g_ple = gain(ks[23], D_MODEL)
    w_ple_gate = nrm(ks[24], (DEPTH, D_MODEL, D_MODEL), F32) * D_MODEL ** -0.5
    return {
        "x_prompt": x_prompt, "x_sample": x_sample,
        "cache_k": cache_k, "cache_v": cache_v, "cache_logf": cache_logf,
        "state_pool": state_pool, "page_table": page_table,
        "p_prompt": p_prompt, "p_sample": p_sample,
        "w_in": w_in, "b_f": b_f, "w_pool": w_pool, "pool_scale": pool_scale,
        "g_fox": g_fox, "g_pool": g_pool, "w_out": w_out,
        "g_pre_mix": g_pre_mix, "g_post_mix": g_post_mix,
        "g_pre_mlp": g_pre_mlp, "g_post_mlp": g_post_mlp,
        "w_up": w_up, "w_down": w_down,
        "w_ple": w_ple, "g_ple": g_ple, "w_ple_gate": w_ple_gate,
    }


def reference(x_prompt, x_sample, cache_k, cache_v, cache_logf, state_pool, page_table,
              p_prompt, p_sample, w_in, b_f, w_pool, pool_scale, g_fox, g_pool, w_out,
              g_pre_mix, g_post_mix, g_pre_mlp, g_post_mlp, w_up, w_down,
              w_ple, g_ple, w_ple_gate):
    B = x_prompt.shape[0]
    DB = x_sample.shape[0]
    n_pages = page_table.shape[1]
    past = n_pages * cache_k.shape[2]
    hp, hs = x_prompt, x_sample
    kps, vps, lfps, sps = [], [], [], []
    kss, vss, lfss, sss = [], [], [], []
    for i in range(DEPTH):
        wts = (w_in[i], b_f[i], w_pool[i], pool_scale[i], g_fox[i], g_pool[i], w_out[i],
               g_pre_mix[i], g_post_mix[i], g_pre_mlp[i], g_post_mlp[i], w_up[i], w_down[i],
               w_ple[i], g_ple[i], w_ple_gate[i])
        pool_hist0 = jnp.zeros((B, POOL_BUF, POOL_WIDTH), x_prompt.dtype)
        hp, kp, vp, lfp, sp = trunk_layer(hp, p_prompt[i], fox_prompt_attend, pool_hist0, 0, *wts)
        kps.append(kp); vps.append(vp); lfps.append(lfp); sps.append(sp)
        k_past = cache_k[i, page_table].reshape(DB, past, N_ATT_HEADS, HEAD_DIM)
        v_past = cache_v[i, page_table].reshape(DB, past, N_ATT_HEADS, HEAD_DIM)
        lf_past = cache_logf[i, page_table].reshape(DB, past, N_ATT_HEADS)
        attend_s = functools.partial(fox_sample_attend, k_past=k_past, v_past=v_past, lf_past=lf_past)
        hs, ks_, vs_, lfs_, ss_ = trunk_layer(hs, p_sample[i], attend_s, state_pool[i], past, *wts)
        kss.append(ks_); vss.append(vs_); lfss.append(lfs_); sss.append(ss_)
    return (hp, hs,
            jnp.stack(kps), jnp.stack(vps), jnp.stack(lfps), jnp.stack(sps),
            jnp.stack(kss), jnp.stack(vss), jnp.stack(lfss), jnp.stack(sss))
```

```python
import functools
import math

import jax
import jax.numpy as jnp
from jax import lax
from jax.experimental import pallas as pl
from jax.experimental.pallas import tpu as pltpu

F32 = jnp.float32
BF16 = jnp.bfloat16

EPS = 1e-6
NEG_INF = -1e30
HEAD_DIM = 64
N_HEADS = 8
ATT_WIDTH = N_HEADS * HEAD_DIM
POOL_WINDOWS = (2, 4, 8, 16)
POOL_GROUP = 128
POOL_WIDTH = POOL_GROUP * len(POOL_WINDOWS)
POOL_BUF = max(POOL_WINDOWS) - 1
HIST_ROWS = POOL_BUF + 1
ATT_SCALE = 1.0 / math.sqrt(HEAD_DIM)

LANES = 128
HEADS_PER_STEP = LANES // HEAD_DIM
VMEM_LIMIT = 56 * 1024 * 1024

NT_DIMS = (((1,), (1,)), ((), ()))


def _params(semantics):
    return pltpu.CompilerParams(dimension_semantics=semantics, vmem_limit_bytes=VMEM_LIMIT)


def _rms(x, g):
    return x * lax.rsqrt(jnp.mean(x * x, axis=-1, keepdims=True) + EPS) * g


def _const_spec(shape):
    zeros = (0,) * len(shape)
    return pl.BlockSpec(shape, lambda *_: zeros, pipeline_mode=pl.Buffered(1))


def _lane_cumsum(x):
    n = x.shape[1]
    lane = lax.broadcasted_iota(jnp.int32, x.shape, 1)
    shift = 1
    while shift < n:
        x = x + jnp.where(lane >= shift, pltpu.roll(x, shift, axis=1), 0.0)
        shift *= 2
    return x


def _split3(x):
    x1 = x.astype(BF16)
    r1 = x - x1.astype(F32)
    x2 = r1.astype(BF16)
    x3 = (r1 - x2.astype(F32)).astype(BF16)
    return x1, x2, x3


def _inproj_kernel(h_ref, g_ref, w_ref, wt_ref, wft_ref, bf_ref, *refs, transposed_kv):
    a = _rms(h_ref[...], g_ref[...]).astype(BF16)
    z = jnp.dot(a, w_ref[...], preferred_element_type=F32)
    w = ATT_WIDTH
    f = lax.dot_general(wft_ref[...], a, NT_DIMS, preferred_element_type=F32) + bf_ref[...]
    lf = jnp.minimum(f, 0.0) - jnp.log1p(jnp.exp(-jnp.abs(f)))
    if transposed_kv:
        q_ref, u_ref, kt_ref, vt_ref, ktb_ref, vtb_ref, lft_ref, ct_ref, carry_ref = refs
        q_ref[...] = (z[:, :w] * ATT_SCALE).astype(BF16)
        u_ref[...] = z[:, w:]
        zt = lax.dot_general(wt_ref[...], a, NT_DIMS, preferred_element_type=F32)
        kt_ref[...] = zt[:w]
        vt_ref[...] = zt[w:]
        ktb_ref[...] = zt[:w].astype(BF16)
        vtb_ref[...] = zt[w:].astype(BF16)
        lft_ref[...] = lf

        @pl.when(pl.program_id(1) == 0)
        def _():
            carry_ref[...] = jnp.zeros_like(carry_ref)

        c = _lane_cumsum(lf) + carry_ref[:, :1]
        ct_ref[...] = c
        carry_ref[...] = jnp.broadcast_to(c[:, -1:], carry_ref.shape)
    else:
        q_ref, k_ref, v_ref, u_ref, lft_ref = refs
        q_ref[...] = (z[:, :w] * ATT_SCALE).astype(BF16)
        k_ref[...] = z[:, w:2 * w]
        v_ref[...] = z[:, 2 * w:3 * w]
        u_ref[...] = z[:, 3 * w:]
        lft_ref[...] = lf


def _inproj(h, g, w_rows, w_t, w_ft, b_f, *, tm, transposed_kv):
    nb, s, d = h.shape
    row = lambda width: pl.BlockSpec((None, tm, width), lambda b, i: (b, i, 0))
    col = lambda height: pl.BlockSpec((None, height, tm), lambda b, i: (b, 0, i))
    sds = jax.ShapeDtypeStruct
    if transposed_kv:
        out_shape = [sds((nb, s, ATT_WIDTH), BF16), sds((nb, s, POOL_WIDTH), F32),
                     sds((nb, ATT_WIDTH, s), F32), sds((nb, ATT_WIDTH, s), F32),
                     sds((nb, ATT_WIDTH, s), BF16), sds((nb, ATT_WIDTH, s), BF16),
                     sds((nb, N_HEADS, s), F32), sds((nb, N_HEADS, s), F32)]
        out_specs = [row(ATT_WIDTH), row(POOL_WIDTH)] + [col(ATT_WIDTH)] * 4 + [col(N_HEADS)] * 2
        scratch = [pltpu.VMEM((N_HEADS, LANES), F32)]
    else:
        out_shape = [sds((nb, s, ATT_WIDTH), BF16), sds((nb, s, ATT_WIDTH), F32), sds((nb, s, ATT_WIDTH), F32),
                     sds((nb, s, POOL_WIDTH), F32), sds((nb, N_HEADS, s), F32)]
        out_specs = [row(ATT_WIDTH)] * 3 + [row(POOL_WIDTH), col(N_HEADS)]
        scratch = []
    return pl.pallas_call(
        functools.partial(_inproj_kernel, transposed_kv=transposed_kv),
        grid=(nb, s // tm),
        in_specs=[row(d), _const_spec((1, d)), _const_spec(w_rows.shape), _const_spec(w_t.shape),
                  _const_spec((N_HEADS, d)), _const_spec((N_HEADS, 1))],
        out_specs=out_specs,
        out_shape=out_shape,
        scratch_shapes=scratch,
        compiler_params=_params(("arbitrary", "arbitrary")),
        name="inproj",
    )(h, g, w_rows, w_t, w_ft, b_f)


def _prompt_attn_kernel(q_ref, kt_ref, vt_ref, c_ref, o_ref, acc_ref, *, tq):
    i = pl.program_id(2)
    first_lanes = lax.broadcasted_iota(jnp.int32, (1, LANES), 1) < HEAD_DIM
    first_rows = lax.broadcasted_iota(jnp.int32, (LANES, 1), 0) < HEAD_DIM
    q2 = q_ref[...]
    zero = jnp.zeros_like(q2)
    q_heads = (jnp.where(first_lanes, q2, zero), jnp.where(first_lanes, zero, q2))
    acc_ref[...] = jnp.zeros_like(acc_ref)

    def block(j, ms, masked):
        start = pl.multiple_of(j * tq, tq)
        kt = kt_ref[:, pl.ds(start, tq)]
        vt = vt_ref[:, pl.ds(start, tq)]
        ck = c_ref[:, pl.ds(start, tq)]
        one = jnp.ones_like(vt)
        vt_aug = (jnp.where(first_rows, vt, one), jnp.where(first_rows, one, vt))
        new_ms = []
        for hh in range(HEADS_PER_STEP):
            s = jnp.dot(q_heads[hh], kt, preferred_element_type=F32) - ck[hh:hh + 1, :]
            if masked:
                r = lax.broadcasted_iota(jnp.int32, s.shape, 0)
                c = lax.broadcasted_iota(jnp.int32, s.shape, 1)
                s = jnp.where(c <= r, s, NEG_INF)
            m_new = jnp.maximum(ms[hh], jnp.max(s, axis=1, keepdims=True))
            alpha = jnp.exp(ms[hh] - m_new)
            p = jnp.exp(s - m_new).astype(BF16)
            pv = lax.dot_general(p, vt_aug[hh], NT_DIMS, preferred_element_type=F32)
            acc_ref[hh] = acc_ref[hh] * alpha + pv
            new_ms.append(m_new)
        return tuple(new_ms)

    m0 = tuple(jnp.full((tq, 1), NEG_INF, F32) for _ in range(HEADS_PER_STEP))
    ms = lax.fori_loop(0, i, lambda j, ms: block(j, ms, False), m0)
    block(i, ms, True)
    acc_a = acc_ref[0]
    acc_b = acc_ref[1]
    out = jnp.where(first_lanes, acc_a / acc_a[:, HEAD_DIM:HEAD_DIM + 1], acc_b / acc_b[:, :1])
    o_ref[...] = out.astype(o_ref.dtype)


def _prompt_attn(q, ktb, vtb, ct, *, tq):
    b, s, _ = q.shape
    n_pairs = N_HEADS // HEADS_PER_STEP
    ct = ct.reshape(b, n_pairs, HEADS_PER_STEP, s)
    qspec = pl.BlockSpec((None, tq, LANES), lambda bi, hp, i: (bi, i, hp))
    kvspec = pl.BlockSpec((None, LANES, s), lambda bi, hp, i: (bi, hp, 0))
    return pl.pallas_call(
        functools.partial(_prompt_attn_kernel, tq=tq),
        grid=(b, n_pairs, s // tq),
        in_specs=[qspec, kvspec, kvspec,
                  pl.BlockSpec((None, None, HEADS_PER_STEP, s), lambda bi, hp, i: (bi, hp, 0, 0))],
        out_specs=qspec,
        out_shape=jax.ShapeDtypeStruct((b, s, ATT_WIDTH), BF16),
        scratch_shapes=[pltpu.VMEM((HEADS_PER_STEP, tq, LANES), F32)],
        compiler_params=_params(("arbitrary", "arbitrary", "arbitrary")),
        name="prompt_attn",
    )(q, ktb, vtb, ct)


def _decay_kernel(pt_ref, lf_hbm, tsuf_ref, ones_ref, psuf_ref, o_ref, x_ref, sem, *, n_pool):
    d = pl.program_id(0)
    b = pl.program_id(1)
    n_pages = x_ref.shape[0]

    def page_copy(p, page):
        return pltpu.make_async_copy(lf_hbm.at[page], x_ref.at[p], sem)

    def start(p, carry):
        page_copy(p, pt_ref[b, p] + d * n_pool).start()
        return carry

    def wait(p, carry):
        page_copy(p, 0).wait()
        return carry

    lax.fori_loop(0, n_pages, start, 0)
    lax.fori_loop(0, n_pages, wait, 0)

    def exact_dot(parts, w, left):
        dots = [jnp.dot(w, t, preferred_element_type=F32) if left else jnp.dot(t, w, preferred_element_type=F32)
                for t in parts]
        return dots[0] + dots[1] + dots[2]

    x = x_ref[...].reshape(n_pages * N_HEADS, x_ref.shape[2])
    xs = _split3(x)
    within = exact_dot(xs, tsuf_ref[...], False)
    totals = exact_dot(xs, ones_ref[...], False)
    later = exact_dot(_split3(totals), psuf_ref[...], True)
    o_ref[...] = (within + later).reshape(o_ref.shape)


def _decay_bias(page_table, logf_pages):
    depth, n_pool, n_heads, page = logf_pages.shape
    db, n_pages = page_table.shape
    t = jnp.arange(page)
    tsuf = (t[:, None] > t[None, :]).astype(BF16)
    ones = jnp.ones((page, page), BF16)
    r = jnp.arange(n_pages * n_heads)
    psuf = ((r[None, :] % n_heads == r[:, None] % n_heads) & (r[None, :] // n_heads > r[:, None] // n_heads)).astype(BF16)
    const = lambda shape: pl.BlockSpec(shape, lambda d, b, pt: (0,) * len(shape), pipeline_mode=pl.Buffered(1))
    return pl.pallas_call(
        functools.partial(_decay_kernel, n_pool=n_pool),
        grid_spec=pltpu.PrefetchScalarGridSpec(
            num_scalar_prefetch=1,
            grid=(depth, db),
            in_specs=[pl.BlockSpec(memory_space=pl.ANY), const(tsuf.shape), const(ones.shape), const(psuf.shape)],
            out_specs=pl.BlockSpec((None, None, n_pages, n_heads, page), lambda d, b, pt: (d, b, 0, 0, 0)),
            scratch_shapes=[pltpu.VMEM((n_pages, n_heads, page), F32), pltpu.SemaphoreType.DMA(())],
        ),
        out_shape=jax.ShapeDtypeStruct((depth, db, n_pages, n_heads, page), F32),
        compiler_params=_params(("arbitrary", "arbitrary")),
        name="decay_bias",
    )(page_table, logf_pages.reshape(depth * n_pool, n_heads, page), tsuf, ones, psuf)


def _expand_heads(x, reps):
    return jnp.concatenate([jnp.broadcast_to(x[h:h + 1, :], (reps, x.shape[1])) for h in range(x.shape[0])], axis=0)


def _sample_attn_kernel(pt_ref, q_ref, *refs, pps, t_new):
    kt_refs = refs[:pps]
    vt_refs = refs[pps:2 * pps]
    bias_ref, kn_ref, vn_ref, lfn_ref, o_ref, qbd_ref, acc_ref, m_ref, l_ref, kpad_ref, vpad_ref = refs[2 * pps:]
    j = pl.program_id(1)

    @pl.when(j == 0)
    def _():
        qt = jnp.concatenate([q_ref[...].astype(F32)] * N_HEADS, axis=0)
        r = lax.broadcasted_iota(jnp.int32, qt.shape, 0) // t_new
        c = lax.broadcasted_iota(jnp.int32, qt.shape, 1) // HEAD_DIM
        qbd_ref[...] = jnp.where(r == c, qt, 0.0).astype(BF16)
        acc_ref[...] = jnp.zeros_like(acc_ref)
        l_ref[...] = jnp.zeros_like(l_ref)
        m_ref[...] = jnp.full_like(m_ref, NEG_INF)

    def update(scores, weighted_values):
        m_prev = m_ref[...]
        m_cur = scores[0].max(axis=1, keepdims=True)
        for s in scores[1:]:
            m_cur = jnp.maximum(m_cur, s.max(axis=1, keepdims=True))
        m_new = jnp.maximum(m_prev, m_cur)
        alpha = jnp.exp(m_prev - m_new)
        l_new = alpha * l_ref[...]
        acc = alpha[:, :1] * acc_ref[...]
        for s, pv in zip(scores, weighted_values):
            p = jnp.exp(s - m_new[:, :1])
            l_new = l_new + p.sum(axis=1, keepdims=True)
            acc = acc + pv(p.astype(BF16))
        m_ref[...] = m_new
        l_ref[...] = l_new
        acc_ref[...] = acc

    qbd = qbd_ref[...]
    scores, weighted_values = [], []
    for r in range(pps):
        s = jnp.dot(qbd, kt_refs[r][...].astype(BF16), preferred_element_type=F32)
        scores.append(s + _expand_heads(bias_ref[r], t_new))
        weighted_values.append(
            lambda p, r=r: lax.dot_general(p, vt_refs[r][...].astype(BF16), NT_DIMS, preferred_element_type=F32))
    update(scores, weighted_values)

    @pl.when(j == pl.num_programs(1) - 1)
    def _():
        kpad_ref[...] = jnp.zeros_like(kpad_ref)
        vpad_ref[...] = jnp.zeros_like(vpad_ref)
        kpad_ref[:t_new, :] = kn_ref[...]
        vpad_ref[:t_new, :] = vn_ref[...]
        cn = _lane_cumsum(lfn_ref[...])
        s = lax.dot_general(qbd, kpad_ref[...].astype(BF16), NT_DIMS, preferred_element_type=F32)
        s = s - _expand_heads(cn, t_new)
        qi = lax.broadcasted_iota(jnp.int32, s.shape, 0) % t_new
        ki = lax.broadcasted_iota(jnp.int32, s.shape, 1)
        s = jnp.where(ki <= qi, s, NEG_INF)
        update([s], [lambda p: jnp.dot(p, vpad_ref[...].astype(BF16), preferred_element_type=F32)])
        out = acc_ref[...] / l_ref[:, :1]
        res = jnp.zeros((t_new, ATT_WIDTH), F32)
        lane_head = lax.broadcasted_iota(jnp.int32, res.shape, 1) // HEAD_DIM
        for h in range(N_HEADS):
            res = jnp.where(lane_head == h, out[h * t_new:(h + 1) * t_new, :], res)
        o_ref[...] = res.astype(o_ref.dtype)


def _sample_attn(page_table, q, k_new, v_new, lf_new, kt_pages, vt_pages, bias, layer, *, pps):
    db, t_new, _ = q.shape
    n_pages = page_table.shape[1]
    page = kt_pages.shape[2]
    n_pool = kt_pages.shape[0] // bias.shape[0]
    base = layer * n_pool
    rows = N_HEADS * t_new

    def page_spec(r):
        return pl.BlockSpec((None, ATT_WIDTH, page), lambda b, j, pt: (pt[b, j * pps + r] + base, 0, 0))

    tok = pl.BlockSpec((None, t_new, ATT_WIDTH), lambda b, j, pt: (b, 0, 0))
    in_specs = ([tok] + [page_spec(r) for r in range(pps)] * 2
                + [pl.BlockSpec((None, None, pps, N_HEADS, page), lambda b, j, pt: (layer, b, j, 0, 0)),
                   tok, tok, pl.BlockSpec((None, N_HEADS, LANES), lambda b, j, pt: (b, 0, 0))])
    return pl.pallas_call(
        functools.partial(_sample_attn_kernel, pps=pps, t_new=t_new),
        grid_spec=pltpu.PrefetchScalarGridSpec(
            num_scalar_prefetch=1,
            grid=(db, n_pages // pps),
            in_specs=in_specs,
            out_specs=tok,
            scratch_shapes=[pltpu.VMEM((rows, ATT_WIDTH), BF16), pltpu.VMEM((rows, ATT_WIDTH), F32),
                            pltpu.VMEM((rows, LANES), F32), pltpu.VMEM((rows, LANES), F32),
                            pltpu.VMEM((page, ATT_WIDTH), F32), pltpu.VMEM((page, ATT_WIDTH), F32)],
        ),
        out_shape=jax.ShapeDtypeStruct((db, t_new, ATT_WIDTH), BF16),
        compiler_params=_params(("arbitrary", "arbitrary")),
        name="sample_attn",
    )(page_table, q, *([kt_pages] * pps), *([vt_pages] * pps), bias, k_new, v_new, lf_new)


def _pool_kernel(*refs, pos0, tm, use_halo):
    if use_halo:
        u_ref, halo_ref, hist_ref, o_ref, z_ref = refs
        prev = jnp.where(pl.program_id(1) == 0, hist_ref[...], halo_ref[...])
    else:
        u_ref, hist_ref, o_ref, z_ref = refs
        prev = hist_ref[...]
    z_ref[:HIST_ROWS, :] = prev
    z_ref[HIST_ROWS:, :] = u_ref[...]
    row = lax.broadcasted_iota(jnp.int32, (tm, 1), 0)
    pos_count = pos0 + pl.program_id(1) * tm + row + 1
    for g, w in enumerate(POOL_WINDOWS):
        cols = slice(g * POOL_GROUP, (g + 1) * POOL_GROUP)
        tok = z_ref[HIST_ROWS:HIST_ROWS + tm, cols]
        total = tok
        for i in range(1, w):
            total = total + z_ref[HIST_ROWS - i:HIST_ROWS - i + tm, cols]
        cnt = jnp.minimum(pos_count, w).astype(F32)
        o_ref[:, cols] = (total / cnt - tok).astype(o_ref.dtype)


def _pool_means(u, hist, *, pos0, tm):
    nb, t, _ = u.shape
    use_halo = t > tm
    row = pl.BlockSpec((None, tm, POOL_WIDTH), lambda b, i: (b, i, 0))
    hist_spec = pl.BlockSpec((None, HIST_ROWS, POOL_WIDTH), lambda b, i: (b, 0, 0))
    per = tm // HIST_ROWS
    halo = pl.BlockSpec((None, HIST_ROWS, POOL_WIDTH), lambda b, i: (b, jnp.maximum(i * per - 1, 0), 0))
    in_specs = [row, halo, hist_spec] if use_halo else [row, hist_spec]
    args = (u, u, hist) if use_halo else (u, hist)
    return pl.pallas_call(
        functools.partial(_pool_kernel, pos0=pos0, tm=tm, use_halo=use_halo),
        grid=(nb, t // tm),
        in_specs=in_specs,
        out_specs=row,
        out_shape=jax.ShapeDtypeStruct(u.shape, BF16),
        scratch_shapes=[pltpu.VMEM((HIST_ROWS + tm, POOL_WIDTH), F32)],
        compiler_params=_params(("arbitrary", "arbitrary")),
        name="pool_means",
    )(*args)


def _post_kernel(h_ref, att_ref, pooled_ref, p_ref, wpool_ref, pscale_ref, gfox_ref, gpool_ref, wout_ref,
                 gpostmix_ref, gpremlp_ref, wup_ref, wdown_ref, gpostmlp_ref, wple_ref, gple_ref, wgate_ref,
                 o_ref, acc_ref, *, ff_chunk):
    dot = functools.partial(jnp.dot, preferred_element_type=F32)
    an = _rms(att_ref[...].astype(F32), gfox_ref[...]).astype(BF16)
    pooled = pooled_ref[...]
    ys = [dot(pooled[:, g * POOL_GROUP:(g + 1) * POOL_GROUP], wpool_ref[g]) for g in range(len(POOL_WINDOWS))]
    y = jnp.concatenate(ys, axis=-1) * pscale_ref[...]
    pn = _rms(y, gpool_ref[...]).astype(BF16)
    mix = dot(an, wout_ref[:ATT_WIDTH, :]) + dot(pn, wout_ref[ATT_WIDTH:, :])
    h1 = h_ref[...] + _rms(mix, gpostmix_ref[...])
    m = _rms(h1, gpremlp_ref[...]).astype(BF16)
    d_ff = wup_ref.shape[1]
    for c in range(d_ff // ff_chunk):
        cols = slice(c * ff_chunk, (c + 1) * ff_chunk)
        hidden = jnp.square(jnp.maximum(dot(m, wup_ref[:, cols]), 0.0)).astype(BF16)
        part = dot(hidden, wdown_ref[cols, :])
        if c == 0:
            acc_ref[...] = part
        else:
            acc_ref[...] += part
    h2 = h1 + _rms(acc_ref[...], gpostmlp_ref[...])
    e = _rms(dot(p_ref[...].astype(BF16), wple_ref[...]), gple_ref[...])
    gate = jax.nn.sigmoid(dot(h2.astype(BF16), wgate_ref[...]))
    o_ref[...] = h2 + gate * e


def _post(h, att, pooled, p, w, *, tm, ff_chunk):
    n, d = h.shape
    row = lambda width: pl.BlockSpec((tm, width), lambda i: (i, 0))
    weights = (w["w_pool"], w["pool_scale"], w["g_fox"], w["g_pool"], w["w_out"], w["g_post_mix"], w["g_pre_mlp"],
               w["w_up"], w["w_down"], w["g_post_mlp"], w["w_ple"], w["g_ple"], w["w_ple_gate"])
    return pl.pallas_call(
        functools.partial(_post_kernel, ff_chunk=ff_chunk),
        grid=(n // tm,),
        in_specs=[row(d), row(ATT_WIDTH), row(POOL_WIDTH), row(p.shape[1])] + [_const_spec(x.shape) for x in weights],
        out_specs=row(d),
        out_shape=jax.ShapeDtypeStruct((n, d), F32),
        scratch_shapes=[pltpu.VMEM((tm, d), F32)],
        compiler_params=_params(("arbitrary",)),
        name="post",
    )(h, att, pooled, p, *weights)


def _tile(n, target):
    t = min(n, target)
    assert n % t == 0, (n, t)
    return t


def kernel(x_prompt, x_sample, cache_k, cache_v, cache_logf, state_pool, page_table, p_prompt, p_sample, w_in, b_f, w_pool, pool_scale, g_fox, g_pool, w_out, g_pre_mix, g_post_mix, g_pre_mlp, g_post_mlp, w_up, w_down, w_ple, g_ple, w_ple_gate):
    b, s, d = x_prompt.shape
    db, t_new, _ = x_sample.shape
    depth, n_pool, page, n_heads, head_dim = cache_k.shape
    n_pages = page_table.shape[1]
    assert (n_heads, head_dim) == (N_HEADS, HEAD_DIM) and w_in.shape[2] == 3 * ATT_WIDTH + POOL_WIDTH + N_HEADS
    assert page == LANES and t_new == 8 and state_pool.shape[2:] == (POOL_BUF, POOL_WIDTH)
    w = ATT_WIDTH
    n_main = 3 * w + POOL_WIDTH
    past = n_pages * page

    w_bf = w_in.astype(BF16)
    w_rows = w_bf[:, :, :n_main]
    w_qu = jnp.concatenate([w_bf[:, :, :w], w_bf[:, :, 3 * w:n_main]], axis=2)
    w_t = jnp.swapaxes(w_bf, 1, 2)
    w_kvt = w_t[:, w:3 * w]
    w_ft = w_t[:, n_main:]
    vec = lambda x: x[:, None, :]
    layers = []
    for i in range(depth):
        layers.append(dict(
            w_pool=w_pool[i].astype(BF16), pool_scale=vec(pool_scale)[i], g_fox=vec(g_fox)[i], g_pool=vec(g_pool)[i],
            w_out=w_out[i].astype(BF16), g_post_mix=vec(g_post_mix)[i], g_pre_mlp=vec(g_pre_mlp)[i],
            w_up=w_up[i].astype(BF16), w_down=w_down[i].astype(BF16), g_post_mlp=vec(g_post_mlp)[i],
            w_ple=w_ple[i].astype(BF16), g_ple=vec(g_ple)[i], w_ple_gate=w_ple_gate[i].astype(BF16)))

    kt_pages = jnp.transpose(cache_k, (0, 1, 3, 4, 2)).reshape(depth * n_pool, w, page)
    vt_pages = jnp.transpose(cache_v, (0, 1, 3, 4, 2)).reshape(depth * n_pool, w, page)
    bias = _decay_bias(page_table, jnp.swapaxes(cache_logf, 2, 3))
    zero_hist = jnp.zeros((b, HIST_ROWS, POOL_WIDTH), F32)

    tm_in = _tile(s, 512)
    tq = _tile(s, 256)
    tm_pool = _tile(s, 512)
    tm_post = _tile(b * s, 512)
    pps = _tile(n_pages, 8)
    n_s = db * t_new
    tok3 = lambda x: x.reshape(db, t_new, x.shape[-1])
    heads_last = lambda xt: jnp.transpose(xt.reshape(b, N_HEADS, HEAD_DIM, s), (0, 3, 1, 2))

    hp = x_prompt
    hs = x_sample.reshape(1, n_s, d)
    outs = [[] for _ in range(8)]
    for i in range(depth):
        g_pre = vec(g_pre_mix)[i]
        bf_col = b_f[i][:, None]
        q, u, kt, vt, ktb, vtb, lft, ct = _inproj(hp, g_pre, w_qu[i], w_kvt[i], w_ft[i], bf_col, tm=tm_in,
                                                  transposed_kv=True)
        att = _prompt_attn(q, ktb, vtb, ct, tq=tq)
        pooled = _pool_means(u, zero_hist, pos0=0, tm=tm_pool)
        hp = _post(hp.reshape(b * s, d), att.reshape(b * s, w), pooled.reshape(b * s, POOL_WIDTH),
                   p_prompt[i].reshape(b * s, -1), layers[i], tm=tm_post, ff_chunk=512).reshape(b, s, d)
        outs[0].append(heads_last(kt))
        outs[1].append(heads_last(vt))
        outs[2].append(jnp.swapaxes(lft, 1, 2))
        outs[3].append(u[:, s - POOL_BUF:, :])
        qs, ks, vs, us, lfts = _inproj(hs, g_pre, w_rows[i], w_kvt[i], w_ft[i], bf_col, tm=n_s, transposed_kv=False)
        lf_s = jnp.swapaxes(lfts[0], 0, 1).reshape(db, t_new, N_HEADS)
        lf_new = jnp.pad(jnp.swapaxes(lf_s, 1, 2), ((0, 0), (0, 0), (0, LANES - t_new)))
        att_s = _sample_attn(page_table, tok3(qs[0]), tok3(ks[0]), tok3(vs[0]), lf_new, kt_pages, vt_pages, bias, i,
                             pps=pps)
        hist = jnp.pad(state_pool[i], ((0, 0), (1, 0), (0, 0)))
        pooled_s = _pool_means(tok3(us[0]), hist, pos0=past, tm=t_new)
        hs = _post(hs.reshape(n_s, d), att_s.reshape(n_s, w), pooled_s.reshape(n_s, POOL_WIDTH),
                   p_sample[i].reshape(n_s, -1), layers[i], tm=n_s, ff_chunk=512).reshape(1, n_s, d)
        outs[4].append(ks.reshape(db, t_new, N_HEADS, HEAD_DIM))
        outs[5].append(vs.reshape(db, t_new, N_HEADS, HEAD_DIM))
        outs[6].append(lf_s)
        outs[7].append(jnp.concatenate([state_pool[i], tok3(us[0])], axis=1)[:, -POOL_BUF:])
    return (hp, hs.reshape(db, t_new, d)) + tuple(jnp.stack(o) for o in outs)
```

```python
import functools
import math

import jax
import jax.numpy as jnp
import numpy as np
from jax import lax
from jax.experimental import pallas as pl
from jax.experimental.pallas import tpu as pltpu

F32 = jnp.float32
BF16 = jnp.bfloat16

EPS = 1e-6
NEG_INF = -1e30
HEAD_DIM = 64
N_HEADS = 8
ATT_WIDTH = N_HEADS * HEAD_DIM
POOL_WINDOWS = (2, 4, 8, 16)
POOL_GROUP = 128
POOL_WIDTH = POOL_GROUP * len(POOL_WINDOWS)
POOL_BUF = max(POOL_WINDOWS) - 1
HIST_ROWS = POOL_BUF + 1
ATT_SCALE = 1.0 / math.sqrt(HEAD_DIM)

LANES = 128
MXU_DIM = 256
HEADS_PER_STEP = LANES // HEAD_DIM
TICKS_PER_TRIP = 4
VMEM_LIMIT = 56 * 1024 * 1024

NT_DIMS = (((1,), (1,)), ((), ()))


def _params(semantics):
    return pltpu.CompilerParams(dimension_semantics=semantics, vmem_limit_bytes=VMEM_LIMIT)


def _rms(x, g):
    return x * lax.rsqrt(jnp.mean(x * x, axis=-1, keepdims=True) + EPS) * g


def _const_spec(shape):
    zeros = (0,) * len(shape)
    return pl.BlockSpec(shape, lambda *_: zeros, pipeline_mode=pl.Buffered(1))


def _lane_cumsum(x):
    n = x.shape[1]
    lane = lax.broadcasted_iota(jnp.int32, x.shape, 1)
    shift = 1
    while shift < n:
        x = x + jnp.where(lane >= shift, pltpu.roll(x, shift, axis=1), 0.0)
        shift *= 2
    return x


def _split3(x):
    x1 = x.astype(BF16)
    r1 = x - x1.astype(F32)
    x2 = r1.astype(BF16)
    x3 = (r1 - x2.astype(F32)).astype(BF16)
    return x1, x2, x3


def _inproj_kernel(h_ref, g_ref, w_ref, wt_ref, wft_ref, bf_ref, *refs, transposed_kv, n_alias=0, layer=None):
    refs = refs[n_alias:]

    def put(ref, val):
        if layer is None:
            ref[...] = val
        else:
            for l in range(ref.shape[0]):
                ref[l] = val if l == layer else jnp.zeros_like(val)

    a = _rms(h_ref[...], g_ref[...]).astype(BF16)
    z = jnp.dot(a, w_ref[...], preferred_element_type=F32)
    w = ATT_WIDTH
    f = lax.dot_general(wft_ref[...], a, NT_DIMS, preferred_element_type=F32) + bf_ref[...]
    lf = jnp.minimum(f, 0.0) - jnp.log1p(jnp.exp(-jnp.abs(f)))
    if transposed_kv:
        q_ref, u_ref, kt_ref, vt_ref, ktb_ref, vtb_ref, lft_ref, ct_ref, carry_ref = refs
        q_ref[...] = (z[:, :w] * ATT_SCALE).astype(BF16)
        u_ref[...] = z[:, w:]
        zt = lax.dot_general(wt_ref[...], a, NT_DIMS, preferred_element_type=F32)
        put(kt_ref, zt[:w])
        put(vt_ref, zt[w:])
        ktb_ref[...] = zt[:w].astype(BF16)
        vtb_ref[...] = zt[w:].astype(BF16)
        put(lft_ref, lf)

        @pl.when(pl.program_id(1) == 0)
        def _():
            carry_ref[...] = jnp.zeros_like(carry_ref)

        c = _lane_cumsum(lf) + carry_ref[:, :1]
        ct_ref[...] = c
        carry_ref[...] = jnp.broadcast_to(c[:, -1:], carry_ref.shape)
    else:
        q_ref, k_ref, v_ref, u_ref, lft_ref = refs
        q_ref[...] = (z[:, :w] * ATT_SCALE).astype(BF16)
        k_ref[...] = z[:, w:2 * w]
        v_ref[...] = z[:, 2 * w:3 * w]
        u_ref[...] = z[:, 3 * w:]
        lft_ref[...] = lf


def _inproj_rows(h, g, w_rows, w_ft, b_f):
    nb, s, d = h.shape
    row = lambda width: pl.BlockSpec((None, s, width), lambda b, i: (b, 0, 0))
    sds = jax.ShapeDtypeStruct
    return pl.pallas_call(
        functools.partial(_inproj_kernel, transposed_kv=False, n_alias=0),
        grid=(nb, 1),
        in_specs=[row(d), _const_spec((1, d)), _const_spec(w_rows.shape), _const_spec(w_ft.shape),
                  _const_spec((N_HEADS, d)), _const_spec((N_HEADS, 1))],
        out_specs=[row(ATT_WIDTH)] * 3 + [row(POOL_WIDTH), pl.BlockSpec((None, N_HEADS, s), lambda b, i: (b, 0, 0))],
        out_shape=[sds((nb, s, ATT_WIDTH), BF16), sds((nb, s, ATT_WIDTH), F32), sds((nb, s, ATT_WIDTH), F32),
                   sds((nb, s, POOL_WIDTH), F32), sds((nb, N_HEADS, s), F32)],
        compiler_params=_params(("arbitrary", "arbitrary")),
        name="inproj_rows",
    )(h, g, w_rows, w_ft, w_ft, b_f)


def _inproj_cols(h, g, w_qu, w_kvt, w_ft, b_f, layer, depth, kv_prev, *, tm):
    nb, s, d = h.shape
    row = lambda width: pl.BlockSpec((None, tm, width), lambda b, i: (b, i, 0))
    col = lambda height: pl.BlockSpec((None, height, tm), lambda b, i: (b, 0, i))
    if kv_prev is None:
        layer_col = lambda height: pl.BlockSpec((depth, None, height, tm), lambda b, i: (0, b, 0, i))
    else:
        layer_col = lambda height: pl.BlockSpec((None, None, height, tm), lambda b, i: (layer, b, 0, i))
    sds = jax.ShapeDtypeStruct
    out_shape = [sds((nb, s, ATT_WIDTH), BF16), sds((nb, s, POOL_WIDTH), F32),
                 sds((depth, nb, ATT_WIDTH, s), F32), sds((depth, nb, ATT_WIDTH, s), F32),
                 sds((nb, ATT_WIDTH, s), BF16), sds((nb, ATT_WIDTH, s), BF16),
                 sds((depth, nb, N_HEADS, s), F32), sds((nb, N_HEADS, s), F32)]
    out_specs = [row(ATT_WIDTH), row(POOL_WIDTH), layer_col(ATT_WIDTH), layer_col(ATT_WIDTH),
                 col(ATT_WIDTH), col(ATT_WIDTH), layer_col(N_HEADS), col(N_HEADS)]
    in_specs = [row(d), _const_spec((1, d)), _const_spec(w_qu.shape), _const_spec(w_kvt.shape),
                _const_spec((N_HEADS, d)), _const_spec((N_HEADS, 1))]
    args = [h, g, w_qu, w_kvt, w_ft, b_f]
    aliases = {}
    if kv_prev is not None:
        for out_idx, buf in zip((2, 3, 6), kv_prev):
            aliases[len(args)] = out_idx
            in_specs.append(pl.BlockSpec(memory_space=pl.ANY))
            args.append(buf)
    return pl.pallas_call(
        functools.partial(_inproj_kernel, transposed_kv=True, n_alias=len(aliases),
                          layer=layer if kv_prev is None else None),
        grid=(nb, s // tm),
        in_specs=in_specs,
        out_specs=out_specs,
        out_shape=out_shape,
        scratch_shapes=[pltpu.VMEM((N_HEADS, LANES), F32)],
        input_output_aliases=aliases,
        compiler_params=_params(("arbitrary", "arbitrary")),
        name="inproj_cols",
    )(*args)


def _prompt_attn_kernel(pi_ref, pj_ref, q_ref, kt_ref, vt_ref, c_ref, o_ref,
                        qs_ref, kta_ref, vta_ref, acc_ref, m_ref, s_ref, p_ref, alpha_ref, *, bq, bk):
    s_len = q_ref.shape[0]
    nq = s_len // bq
    per = bk // bq
    first_lanes = lax.broadcasted_iota(jnp.int32, (1, LANES), 1) < HEAD_DIM
    first_rows = lax.broadcasted_iota(jnp.int32, (LANES, 1), 0) < HEAD_DIM

    q2 = q_ref[...].reshape(nq, bq, LANES)
    zero = jnp.zeros_like(q2)
    lane = lax.broadcasted_iota(jnp.int32, (bq, LANES), 1)
    ones_a = jnp.where(lane < 3, 1.0, 0.0).astype(BF16)
    ones_b = jnp.where((lane >= 3) & (lane < 6), 1.0, 0.0).astype(BF16)
    first_q = (lane < HEAD_DIM)[None]
    qs_ref[:, :bq, :LANES] = jnp.where(first_q, q2, zero)
    qs_ref[:, bq:, :LANES] = jnp.where(first_q, zero, q2)
    qs_ref[:, :bq, LANES:] = jnp.broadcast_to(ones_a[None], (nq, bq, LANES))
    qs_ref[:, bq:, LANES:] = jnp.broadcast_to(ones_b[None], (nq, bq, LANES))
    kta_ref[:LANES, :] = kt_ref[...]
    neg_c = -c_ref[...]
    parts = _split3(neg_c[0:1, :]) + _split3(neg_c[1:2, :])
    r16 = lax.broadcasted_iota(jnp.int32, (16, s_len), 0)
    extra = jnp.zeros((16, s_len), F32)
    for idx, part in enumerate(parts):
        extra = jnp.where(r16 == idx, part.astype(F32), extra)
    kta_ref[LANES:LANES + 16, :] = extra.astype(BF16)
    kta_ref[LANES + 16:, :] = jnp.zeros((LANES - 16, s_len), BF16)
    vt = vt_ref[...]
    one = jnp.ones_like(vt)
    vta_ref[0] = jnp.where(first_rows, vt, one)
    vta_ref[1] = jnp.where(first_rows, one, vt)
    acc_ref[...] = jnp.zeros_like(acc_ref)
    m_ref[...] = jnp.full_like(m_ref, NEG_INF)

    def scores(i, j, slot, masked):
        cols = pl.ds(pl.multiple_of(j * bk, bk), bk)
        s2 = jnp.dot(qs_ref[i], kta_ref[:, cols], preferred_element_type=F32)
        if masked:
            r = (lax.broadcasted_iota(jnp.int32, s2.shape, 0) & (bq - 1)) + (i * bq - j * bk)
            c = lax.broadcasted_iota(jnp.int32, s2.shape, 1)
            s2 = jnp.where(c <= r, s2, NEG_INF)
        s_ref[slot] = s2

    def softmax(i, slot):
        rows = pl.ds(pl.multiple_of(i * bq, bq), bq)
        for hh in range(HEADS_PER_STEP):
            s = s_ref[slot, hh * bq:(hh + 1) * bq, :]
            m_prev = m_ref[hh, rows, :]
            m_new = jnp.maximum(m_prev, jnp.max(s, axis=1, keepdims=True))
            alpha_ref[slot, hh] = jnp.exp(m_prev - m_new)
            p_ref[slot, hh] = jnp.concatenate(
                [jnp.exp(s[:, k * LANES:(k + 1) * LANES] - m_new) for k in range(bk // LANES)], axis=1).astype(BF16)
            m_ref[hh, rows, :] = m_new

    def accumulate(i, j, slot):
        rows = pl.ds(pl.multiple_of(i * bq, bq), bq)
        cols = pl.ds(pl.multiple_of(j * bk, bk), bk)
        for hh in range(HEADS_PER_STEP):
            pv = lax.dot_general(p_ref[slot, hh], vta_ref[hh, :, cols], NT_DIMS, preferred_element_type=F32)
            acc_ref[hh, rows, :] = acc_ref[hh, rows, :] * alpha_ref[slot, hh] + pv

    def sweep(n, pair, masked):
        if n == 0:
            return
        scores(*pair(0), 0, masked)
        if n == 1:
            softmax(pair(0)[0], 0)
            accumulate(*pair(0), 0)
            return
        softmax(pair(0)[0], 0)
        scores(*pair(1), 1, masked)

        def tick(t, slot):
            accumulate(*pair(t - 2), slot)
            softmax(pair(t - 1)[0], 1 - slot)
            scores(*pair(t), slot, masked)

        def ticks(k, carry):
            for x in range(TICKS_PER_TRIP):
                tick(TICKS_PER_TRIP * k + 2 + x, x % 2)
            return carry

        trips = (n - 2) // TICKS_PER_TRIP
        lax.fori_loop(0, trips, ticks, 0)
        for t in range(2 + trips * TICKS_PER_TRIP, n):
            tick(t, t % 2)
        accumulate(*pair(n - 2), n % 2)
        softmax(pair(n - 1)[0], (n - 1) % 2)
        accumulate(*pair(n - 1), (n - 1) % 2)

    def last_key_block(i):
        return i // per if isinstance(i, int) else lax.shift_right_logical(i, per.bit_length() - 1)

    sweep(nq, lambda t: (t, last_key_block(t)), True)
    sweep(sum(i // per for i in range(nq)), lambda t: (pi_ref[t], pj_ref[t]), False)

    def finish(i, carry):
        rows = pl.ds(pl.multiple_of(i * bq, bq), bq)
        a = acc_ref[0, rows, :]
        b = acc_ref[1, rows, :]
        o_ref[rows, :] = jnp.where(first_lanes, a / a[:, HEAD_DIM:HEAD_DIM + 1], b / b[:, :1]).astype(o_ref.dtype)
        return carry

    lax.fori_loop(0, nq, finish, 0)


def _prompt_attn(q, ktb, vtb, ct, *, bq, bk):
    b, s, _ = q.shape
    nq = s // bq
    per = bk // bq
    assert s % bk == 0 and bk % bq == 0 and per & (per - 1) == 0 and bq & (bq - 1) == 0
    n_pairs = N_HEADS // HEADS_PER_STEP
    ct = ct.reshape(b, n_pairs, HEADS_PER_STEP, s)
    below = [(i, j) for i in range(nq) for j in range(i // per)] or [(0, 0)]
    pi = jnp.asarray(np.array([p[0] for p in below], np.int32))
    pj = jnp.asarray(np.array([p[1] for p in below], np.int32))
    qspec = pl.BlockSpec((None, s, LANES), lambda bi, hp, *_: (bi, 0, hp))
    kvspec = pl.BlockSpec((None, LANES, s), lambda bi, hp, *_: (bi, hp, 0))
    return pl.pallas_call(
        functools.partial(_prompt_attn_kernel, bq=bq, bk=bk),
        grid_spec=pltpu.PrefetchScalarGridSpec(
            num_scalar_prefetch=2,
            grid=(b, n_pairs),
            in_specs=[qspec, kvspec, kvspec,
                      pl.BlockSpec((None, None, HEADS_PER_STEP, s), lambda bi, hp, *_: (bi, hp, 0, 0))],
            out_specs=qspec,
            scratch_shapes=[pltpu.VMEM((nq, HEADS_PER_STEP * bq, MXU_DIM), BF16),
                            pltpu.VMEM((MXU_DIM, s), BF16),
                            pltpu.VMEM((HEADS_PER_STEP, LANES, s), BF16),
                            pltpu.VMEM((HEADS_PER_STEP, s, LANES), F32),
                            pltpu.VMEM((HEADS_PER_STEP, s, LANES), F32),
                            pltpu.VMEM((2, HEADS_PER_STEP * bq, bk), F32),
                            pltpu.VMEM((2, HEADS_PER_STEP, bq, bk), BF16),
                            pltpu.VMEM((2, HEADS_PER_STEP, bq, LANES), F32)],
        ),
        out_shape=jax.ShapeDtypeStruct((b, s, ATT_WIDTH), BF16),
        compiler_params=_params(("arbitrary", "arbitrary")),
        name="prompt_attn",
    )(pi, pj, q, ktb, vtb, ct)


def _decay_kernel(pt_ref, lf_hbm, wcat_ref, psuf_ref, o_ref, x_ref, sem, *, n_pool):
    d = pl.program_id(0)
    b = pl.program_id(1)
    nb = pl.num_programs(1)
    step = d * nb + b
    slot = step & 1
    n_pages = x_ref.shape[2]

    def page_copy(p, page, slot_):
        return pltpu.make_async_copy(lf_hbm.at[page], x_ref.at[slot_, :, pl.ds(p, 1), :], sem.at[slot_])

    def fetch(d_, b_, slot_):
        def start(p, carry):
            page_copy(p, pt_ref[b_, p] + d_ * n_pool, slot_).start()
            return carry
        lax.fori_loop(0, n_pages, start, 0, unroll=8)

    @pl.when(step == 0)
    def _():
        fetch(d, b, slot)

    @pl.when(step + 1 < pl.num_programs(0) * nb)
    def _():
        wrap = b + 1 == nb
        fetch(jnp.where(wrap, d + 1, d), jnp.where(wrap, 0, b + 1), 1 - slot)

    def wait(p, carry):
        page_copy(p, 0, slot).wait()
        return carry

    lax.fori_loop(0, n_pages, wait, 0, unroll=8)

    page = x_ref.shape[3]
    xs = _split3(x_ref[slot].reshape(N_HEADS * n_pages, page))
    wc = sum(jnp.dot(t, wcat_ref[...], preferred_element_type=F32) for t in xs)
    head = lambda h: slice(h * n_pages, (h + 1) * n_pages)
    totals = jnp.concatenate([wc[head(h), page:] for h in range(N_HEADS)], axis=1)
    later = sum(jnp.dot(psuf_ref[...], t, preferred_element_type=F32) for t in _split3(totals))
    for h in range(N_HEADS):
        o_ref[h] = wc[head(h), :page] + later[:, h * page:(h + 1) * page]


def _decay_bias(page_table, logf_pages):
    depth, n_pool, n_heads, page = logf_pages.shape
    db, n_pages = page_table.shape
    t = jnp.arange(page)
    tsuf = (t[:, None] > t[None, :]).astype(BF16)
    wcat = jnp.concatenate([tsuf, jnp.ones((page, page), BF16)], axis=1)
    pidx = jnp.arange(n_pages)
    psuf = (pidx[None, :] > pidx[:, None]).astype(BF16)
    const = lambda shape: pl.BlockSpec(shape, lambda d, b, pt: (0,) * len(shape), pipeline_mode=pl.Buffered(1))
    return pl.pallas_call(
        functools.partial(_decay_kernel, n_pool=n_pool),
        grid_spec=pltpu.PrefetchScalarGridSpec(
            num_scalar_prefetch=1,
            grid=(depth, db),
            in_specs=[pl.BlockSpec(memory_space=pl.ANY), const(wcat.shape), const(psuf.shape)],
            out_specs=pl.BlockSpec((None, None, n_heads, n_pages, page), lambda d, b, pt: (d, b, 0, 0, 0)),
            scratch_shapes=[pltpu.VMEM((2, n_heads, n_pages, page), F32), pltpu.SemaphoreType.DMA((2,))],
        ),
        out_shape=jax.ShapeDtypeStruct((depth, db, n_heads, n_pages, page), F32),
        compiler_params=_params(("arbitrary", "arbitrary")),
        name="decay_bias",
    )(page_table, logf_pages.reshape(depth * n_pool, n_heads, 1, page), wcat, psuf)


def _expand_heads(row_of_head, reps):
    return jnp.concatenate([jnp.broadcast_to(row_of_head(h), (reps, row_of_head(h).shape[1]))
                            for h in range(N_HEADS)], axis=0)


def _sample_attn_kernel(pt_ref, q_ref, *refs, pps, t_new):
    kt_refs = refs[:pps]
    vt_refs = refs[pps:2 * pps]
    bias_ref, kn_ref, vn_ref, lfn_ref, o_ref, qbd_ref, acc_ref, m_ref, l_ref, s_ref, kpad_ref, vpad_ref = refs[2 * pps:]
    j = pl.program_id(1)

    @pl.when(j == 0)
    def _():
        qt = jnp.concatenate([q_ref[...].astype(F32)] * N_HEADS, axis=0)
        r = lax.broadcasted_iota(jnp.int32, qt.shape, 0) // t_new
        c = lax.broadcasted_iota(jnp.int32, qt.shape, 1) // HEAD_DIM
        qbd_ref[...] = jnp.where(r == c, qt, 0.0).astype(BF16)
        acc_ref[...] = jnp.zeros_like(acc_ref)
        l_ref[...] = jnp.zeros_like(l_ref)
        m_ref[...] = jnp.full_like(m_ref, NEG_INF)
        s_ref[...] = jnp.full_like(s_ref, NEG_INF)

    def update(scores, weighted_values):
        m_prev = m_ref[...]
        m_cur = scores[0].max(axis=1, keepdims=True)
        for s in scores[1:]:
            m_cur = jnp.maximum(m_cur, s.max(axis=1, keepdims=True))
        m_new = jnp.maximum(m_prev, m_cur)
        alpha = jnp.exp(m_prev - m_new)
        l_new = alpha * l_ref[...]
        acc = alpha[:, :1] * acc_ref[...]
        for s, pv in zip(scores, weighted_values):
            p = jnp.exp(s - m_new[:, :1])
            l_new = l_new + p.sum(axis=1, keepdims=True)
            acc = acc + pv(p.astype(BF16))
        m_ref[...] = m_new
        l_ref[...] = l_new
        acc_ref[...] = acc

    waiting = [s_ref[r] for r in range(pps)]
    qbd = qbd_ref[...]
    for r in range(pps):
        s = jnp.dot(qbd, kt_refs[r][...].astype(BF16), preferred_element_type=F32)
        s_ref[r] = s + _expand_heads(lambda h: bias_ref[h, r:r + 1, :], t_new)
    update(waiting,
           [lambda p, r=r: lax.dot_general(p, vt_refs[r][...].astype(BF16), NT_DIMS, preferred_element_type=F32)
            for r in range(pps)])

    @pl.when(j == pl.num_programs(1) - 1)
    def _():
        kpad_ref[...] = jnp.zeros_like(kpad_ref)
        vpad_ref[...] = jnp.zeros_like(vpad_ref)
        kpad_ref[:t_new, :] = kn_ref[...]
        vpad_ref[:t_new, :] = vn_ref[...]
        cn = _lane_cumsum(lfn_ref[...])
        s = lax.dot_general(qbd, kpad_ref[...].astype(BF16), NT_DIMS, preferred_element_type=F32)
        s = s - _expand_heads(lambda h: cn[h:h + 1, :], t_new)
        qi = lax.broadcasted_iota(jnp.int32, s.shape, 0) % t_new
        ki = lax.broadcasted_iota(jnp.int32, s.shape, 1)
        s = jnp.where(ki <= qi, s, NEG_INF)
        update([s], [lambda p: jnp.dot(p, vpad_ref[...].astype(BF16), preferred_element_type=F32)])
        out = acc_ref[...] / l_ref[:, :1]
        res = jnp.zeros((t_new, ATT_WIDTH), F32)
        lane_head = lax.broadcasted_iota(jnp.int32, res.shape, 1) // HEAD_DIM
        for h in range(N_HEADS):
            res = jnp.where(lane_head == h, out[h * t_new:(h + 1) * t_new, :], res)
        o_ref[...] = res.astype(o_ref.dtype)


def _sample_attn(page_table, q, k_new, v_new, lf_new, kt_pages, vt_pages, bias, layer, *, pps):
    db, t_new, _ = q.shape
    n_pages = page_table.shape[1]
    page = kt_pages.shape[2]
    n_pool = kt_pages.shape[0] // bias.shape[0]
    base = layer * n_pool
    rows = N_HEADS * t_new
    n_steps = n_pages // pps
    key_step = lambda j: jnp.minimum(j, n_steps - 1)
    value_step = lambda j: jnp.maximum(j - 1, 0)

    def page_spec(r, step_of):
        return pl.BlockSpec((None, ATT_WIDTH, page), lambda b, j, pt: (pt[b, step_of(j) * pps + r] + base, 0, 0))

    tok = pl.BlockSpec((None, t_new, ATT_WIDTH), lambda b, j, pt: (b, 0, 0))
    in_specs = ([tok] + [page_spec(r, key_step) for r in range(pps)] + [page_spec(r, value_step) for r in range(pps)]
                + [pl.BlockSpec((None, None, N_HEADS, pps, page), lambda b, j, pt: (layer, b, 0, key_step(j), 0)),
                   tok, tok, pl.BlockSpec((None, N_HEADS, LANES), lambda b, j, pt: (b, 0, 0))])
    return pl.pallas_call(
        functools.partial(_sample_attn_kernel, pps=pps, t_new=t_new),
        grid_spec=pltpu.PrefetchScalarGridSpec(
            num_scalar_prefetch=1,
            grid=(db, n_steps + 1),
            in_specs=in_specs,
            out_specs=tok,
            scratch_shapes=[pltpu.VMEM((rows, ATT_WIDTH), BF16), pltpu.VMEM((rows, ATT_WIDTH), F32),
                            pltpu.VMEM((rows, LANES), F32), pltpu.VMEM((rows, LANES), F32),
                            pltpu.VMEM((pps, rows, page), F32),
                            pltpu.VMEM((page, ATT_WIDTH), F32), pltpu.VMEM((page, ATT_WIDTH), F32)],
        ),
        out_shape=jax.ShapeDtypeStruct((db, t_new, ATT_WIDTH), BF16),
        compiler_params=_params(("arbitrary", "arbitrary")),
        name="sample_attn",
    )(page_table, q, *([kt_pages] * pps), *([vt_pages] * pps), bias, k_new, v_new, lf_new)


def _pool_kernel(*refs, pos0, tm, use_halo):
    if use_halo:
        u_ref, halo_ref, hist_ref, o_ref, z_ref = refs
        prev = jnp.where(pl.program_id(1) == 0, hist_ref[...], halo_ref[...])
    else:
        u_ref, hist_ref, o_ref, z_ref = refs
        prev = hist_ref[...]
    z_ref[:HIST_ROWS, :] = prev
    z_ref[HIST_ROWS:, :] = u_ref[...]
    row = lax.broadcasted_iota(jnp.int32, (tm, 1), 0)
    pos_count = pos0 + pl.program_id(1) * tm + row + 1
    for g, w in enumerate(POOL_WINDOWS):
        cols = slice(g * POOL_GROUP, (g + 1) * POOL_GROUP)
        total = z_ref[:, cols]
        shift = 1
        while shift < w:
            total = total + pltpu.roll(total, shift, axis=0)
            shift *= 2
        cnt = jnp.minimum(pos_count, w).astype(F32)
        o_ref[:, cols] = (total[HIST_ROWS:, :] / cnt - z_ref[HIST_ROWS:, cols]).astype(o_ref.dtype)


def _pool_means(u, hist, *, pos0, tm):
    nb, t, _ = u.shape
    use_halo = t > tm
    row = pl.BlockSpec((None, tm, POOL_WIDTH), lambda b, i: (b, i, 0))
    hist_spec = pl.BlockSpec((None, HIST_ROWS, POOL_WIDTH), lambda b, i: (b, 0, 0))
    per = tm // HIST_ROWS
    halo = pl.BlockSpec((None, HIST_ROWS, POOL_WIDTH), lambda b, i: (b, jnp.maximum(i * per - 1, 0), 0))
    in_specs = [row, halo, hist_spec] if use_halo else [row, hist_spec]
    args = (u, u, hist) if use_halo else (u, hist)
    return pl.pallas_call(
        functools.partial(_pool_kernel, pos0=pos0, tm=tm, use_halo=use_halo),
        grid=(nb, t // tm),
        in_specs=in_specs,
        out_specs=row,
        out_shape=jax.ShapeDtypeStruct(u.shape, BF16),
        scratch_shapes=[pltpu.VMEM((HIST_ROWS + tm, POOL_WIDTH), F32)],
        compiler_params=_params(("arbitrary", "arbitrary")),
        name="pool_means",
    )(*args)


def _post_kernel(h_ref, att_ref, pooled_ref, p_ref, wpool_ref, pscale_ref, gfox_ref, gpool_ref, wout_ref,
                 gpostmix_ref, gpremlp_ref, wup_ref, wdown_ref, gpostmlp_ref, wple_ref, gple_ref, wgate_ref,
                 o_ref, acc_ref, *, ff_chunk):
    dot = functools.partial(jnp.dot, preferred_element_type=F32)
    an = _rms(att_ref[...].astype(F32), gfox_ref[...]).astype(BF16)
    pooled = pooled_ref[...]
    ys = [dot(pooled[:, g * POOL_GROUP:(g + 1) * POOL_GROUP], wpool_ref[g]) for g in range(len(POOL_WINDOWS))]
    y = jnp.concatenate(ys, axis=-1) * pscale_ref[...]
    pn = _rms(y, gpool_ref[...]).astype(BF16)
    mix = dot(an, wout_ref[:ATT_WIDTH, :]) + dot(pn, wout_ref[ATT_WIDTH:, :])
    h1 = h_ref[...] + _rms(mix, gpostmix_ref[...])
    m = _rms(h1, gpremlp_ref[...]).astype(BF16)
    d_ff = wup_ref.shape[1]
    for c in range(d_ff // ff_chunk):
        cols = slice(c * ff_chunk, (c + 1) * ff_chunk)
        hidden = jnp.square(jnp.maximum(dot(m, wup_ref[:, cols]), 0.0)).astype(BF16)
        part = dot(hidden, wdown_ref[cols, :])
        if c == 0:
            acc_ref[...] = part
        else:
            acc_ref[...] += part
    h2 = h1 + _rms(acc_ref[...], gpostmlp_ref[...])
    e = _rms(dot(p_ref[...].astype(BF16), wple_ref[...]), gple_ref[...])
    gate = jax.nn.sigmoid(dot(h2.astype(BF16), wgate_ref[...]))
    o_ref[...] = h2 + gate * e


def _post(h, att, pooled, p, w, *, tm, ff_chunk):
    n, d = h.shape
    row = lambda width: pl.BlockSpec((tm, width), lambda i: (i, 0))
    weights = (w["w_pool"], w["pool_scale"], w["g_fox"], w["g_pool"], w["w_out"], w["g_post_mix"], w["g_pre_mlp"],
               w["w_up"], w["w_down"], w["g_post_mlp"], w["w_ple"], w["g_ple"], w["w_ple_gate"])
    return pl.pallas_call(
        functools.partial(_post_kernel, ff_chunk=ff_chunk),
        grid=(n // tm,),
        in_specs=[row(d), row(ATT_WIDTH), row(POOL_WIDTH), row(p.shape[1])] + [_const_spec(x.shape) for x in weights],
        out_specs=row(d),
        out_shape=jax.ShapeDtypeStruct((n, d), F32),
        scratch_shapes=[pltpu.VMEM((tm, d), F32)],
        compiler_params=_params(("arbitrary",)),
        name="post",
    )(h, att, pooled, p, *weights)


def _tile(n, target):
    t = min(n, target)
    assert n % t == 0, (n, t)
    return t


def kernel(x_prompt, x_sample, cache_k, cache_v, cache_logf, state_pool, page_table, p_prompt, p_sample, w_in, b_f, w_pool, pool_scale, g_fox, g_pool, w_out, g_pre_mix, g_post_mix, g_pre_mlp, g_post_mlp, w_up, w_down, w_ple, g_ple, w_ple_gate):
    b, s, d = x_prompt.shape
    db, t_new, _ = x_sample.shape
    depth, n_pool, page, n_heads, head_dim = cache_k.shape
    n_pages = page_table.shape[1]
    assert (n_heads, head_dim) == (N_HEADS, HEAD_DIM) and w_in.shape[2] == 3 * ATT_WIDTH + POOL_WIDTH + N_HEADS
    assert page == LANES and t_new == 8 and state_pool.shape[2:] == (POOL_BUF, POOL_WIDTH)
    w = ATT_WIDTH
    n_main = 3 * w + POOL_WIDTH
    past = n_pages * page

    w_bf = w_in.astype(BF16)
    w_rows = w_bf[:, :, :n_main]
    w_qu = jnp.concatenate([w_bf[:, :, :w], w_bf[:, :, 3 * w:n_main]], axis=2)
    w_t = jnp.swapaxes(w_bf, 1, 2)
    w_kvt = w_t[:, w:3 * w]
    w_ft = w_t[:, n_main:]
    vec = lambda x: x[:, None, :]
    layers = []
    for i in range(depth):
        layers.append(dict(
            w_pool=w_pool[i].astype(BF16), pool_scale=vec(pool_scale)[i], g_fox=vec(g_fox)[i], g_pool=vec(g_pool)[i],
            w_out=w_out[i].astype(BF16), g_post_mix=vec(g_post_mix)[i], g_pre_mlp=vec(g_pre_mlp)[i],
            w_up=w_up[i].astype(BF16), w_down=w_down[i].astype(BF16), g_post_mlp=vec(g_post_mlp)[i],
            w_ple=w_ple[i].astype(BF16), g_ple=vec(g_ple)[i], w_ple_gate=w_ple_gate[i].astype(BF16)))

    kt_pages = jnp.transpose(cache_k, (0, 1, 3, 4, 2)).reshape(depth * n_pool, w, page)
    vt_pages = jnp.transpose(cache_v, (0, 1, 3, 4, 2)).reshape(depth * n_pool, w, page)
    bias = _decay_bias(page_table, jnp.swapaxes(cache_logf, 2, 3))
    zero_hist = jnp.zeros((b, HIST_ROWS, POOL_WIDTH), F32)

    tm_in = _tile(s, 512)
    bq = _tile(s, MXU_DIM)
    bk = _tile(s, 2 * MXU_DIM)
    tm_pool = _tile(s, 512)
    tm_post = _tile(b * s, 512)
    pps = _tile(n_pages, 8)
    n_s = db * t_new
    tok3 = lambda x: x.reshape(db, t_new, x.shape[-1])

    hp = x_prompt
    hs = x_sample.reshape(1, n_s, d)
    kv_bufs = None
    pool_p, k_s, v_s, lf_s_all, pool_s = [], [], [], [], []
    for i in range(depth):
        g_pre = vec(g_pre_mix)[i]
        bf_col = b_f[i][:, None]
        q, u, kt_all, vt_all, ktb, vtb, lft_all, ct = _inproj_cols(hp, g_pre, w_qu[i], w_kvt[i], w_ft[i], bf_col, i, depth,
                                                                   kv_bufs, tm=tm_in)
        kv_bufs = (kt_all, vt_all, lft_all)
        att = _prompt_attn(q, ktb, vtb, ct, bq=bq, bk=bk)
        pooled = _pool_means(u, zero_hist, pos0=0, tm=tm_pool)
        hp = _post(hp.reshape(b * s, d), att.reshape(b * s, w), pooled.reshape(b * s, POOL_WIDTH),
                   p_prompt[i].reshape(b * s, -1), layers[i], tm=tm_post, ff_chunk=512).reshape(b, s, d)
        pool_p.append(u[:, s - POOL_BUF:, :])
        qs, ks, vs, us, lfts = _inproj_rows(hs, g_pre, w_rows[i], w_ft[i], bf_col)
        lf_s = jnp.swapaxes(lfts[0], 0, 1).reshape(db, t_new, N_HEADS)
        lf_new = jnp.pad(jnp.swapaxes(lf_s, 1, 2), ((0, 0), (0, 0), (0, LANES - t_new)))
        att_s = _sample_attn(page_table, tok3(qs[0]), tok3(ks[0]), tok3(vs[0]), lf_new, kt_pages, vt_pages, bias, i,
                             pps=pps)
        hist = jnp.pad(state_pool[i], ((0, 0), (1, 0), (0, 0)))
        pooled_s = _pool_means(tok3(us[0]), hist, pos0=past, tm=t_new)
        hs = _post(hs.reshape(n_s, d), att_s.reshape(n_s, w), pooled_s.reshape(n_s, POOL_WIDTH),
                   p_sample[i].reshape(n_s, -1), layers[i], tm=n_s, ff_chunk=512).reshape(1, n_s, d)
        k_s.append(ks.reshape(db, t_new, N_HEADS, HEAD_DIM))
        v_s.append(vs.reshape(db, t_new, N_HEADS, HEAD_DIM))
        lf_s_all.append(lf_s)
        pool_s.append(jnp.concatenate([state_pool[i], tok3(us[0])], axis=1)[:, -POOL_BUF:])
    kt_all, vt_all, lft_all = kv_bufs
    heads_last = lambda xt: jnp.transpose(xt.reshape(depth, b, N_HEADS, HEAD_DIM, s), (0, 1, 4, 2, 3))
    return (hp, hs.reshape(db, t_new, d), heads_last(kt_all), heads_last(vt_all), jnp.swapaxes(lft_all, 2, 3),
            jnp.stack(pool_p), jnp.stack(k_s), jnp.stack(v_s), jnp.stack(lf_s_all), jnp.stack(pool_s))
```

```python
import functools
import math

import jax
import jax.numpy as jnp
import numpy as np
from jax import lax
from jax.experimental import pallas as pl
from jax.experimental.pallas import tpu as pltpu

F32 = jnp.float32
BF16 = jnp.bfloat16

EPS = 1e-6
NEG_INF = -1e30
HEAD_DIM = 64
N_HEADS = 8
ATT_WIDTH = N_HEADS * HEAD_DIM
POOL_WINDOWS = (2, 4, 8, 16)
POOL_GROUP = 128
POOL_WIDTH = POOL_GROUP * len(POOL_WINDOWS)
POOL_BUF = max(POOL_WINDOWS) - 1
HIST_ROWS = POOL_BUF + 1
ATT_SCALE = 1.0 / math.sqrt(HEAD_DIM)

LANES = 128
MXU_DIM = 256
HEADS_PER_STEP = LANES // HEAD_DIM
MAX_TICKS_PER_TRIP = 6
VMEM_LIMIT = 56 * 1024 * 1024

NT_DIMS = (((1,), (1,)), ((), ()))


def _params(semantics):
    return pltpu.CompilerParams(dimension_semantics=semantics, vmem_limit_bytes=VMEM_LIMIT)


def _rms(x, g):
    return x * lax.rsqrt(jnp.mean(x * x, axis=-1, keepdims=True) + EPS) * g


def _const_spec(shape):
    zeros = (0,) * len(shape)
    return pl.BlockSpec(shape, lambda *_: zeros, pipeline_mode=pl.Buffered(1))


def _lane_cumsum(x):
    n = x.shape[1]
    lane = lax.broadcasted_iota(jnp.int32, x.shape, 1)
    shift = 1
    while shift < n:
        x = x + jnp.where(lane >= shift, pltpu.roll(x, shift, axis=1), 0.0)
        shift *= 2
    return x


def _split3(x):
    x1 = x.astype(BF16)
    r1 = x - x1.astype(F32)
    x2 = r1.astype(BF16)
    x3 = (r1 - x2.astype(F32)).astype(BF16)
    return x1, x2, x3


def _inproj_kernel(h_ref, g_ref, w_ref, wt_ref, wft_ref, bf_ref, *refs, transposed_kv, n_alias=0, layer=None):
    refs = refs[n_alias:]

    def put(ref, val):
        if layer is None:
            ref[...] = val
        else:
            for l in range(ref.shape[0]):
                ref[l] = val if l == layer else jnp.zeros_like(val)

    a = _rms(h_ref[...], g_ref[...]).astype(BF16)
    z = jnp.dot(a, w_ref[...], preferred_element_type=F32)
    w = ATT_WIDTH
    f = lax.dot_general(wft_ref[...], a, NT_DIMS, preferred_element_type=F32) + bf_ref[...]
    lf = jnp.minimum(f, 0.0) - jnp.log1p(jnp.exp(-jnp.abs(f)))
    if transposed_kv:
        q_ref, u_ref, kt_ref, vt_ref, ktb_ref, vtb_ref, lft_ref, ct_ref, carry_ref = refs
        q_ref[...] = (z[:, :w] * ATT_SCALE).astype(BF16)
        u_ref[...] = z[:, w:]
        zt = lax.dot_general(wt_ref[...], a, NT_DIMS, preferred_element_type=F32)
        put(kt_ref, zt[:w])
        put(vt_ref, zt[w:])
        ktb_ref[...] = zt[:w].astype(BF16)
        vtb_ref[...] = zt[w:].astype(BF16)
        put(lft_ref, lf)

        @pl.when(pl.program_id(1) == 0)
        def _():
            carry_ref[...] = jnp.zeros_like(carry_ref)

        c = _lane_cumsum(lf) + carry_ref[:, :1]
        ct_ref[...] = c
        carry_ref[...] = jnp.broadcast_to(c[:, -1:], carry_ref.shape)
    else:
        q_ref, k_ref, v_ref, u_ref, lft_ref = refs
        q_ref[...] = (z[:, :w] * ATT_SCALE).astype(BF16)
        k_ref[...] = z[:, w:2 * w]
        v_ref[...] = z[:, 2 * w:3 * w]
        u_ref[...] = z[:, 3 * w:]
        lft_ref[...] = lf


def _inproj_rows(h, g, w_rows, w_ft, b_f):
    nb, s, d = h.shape
    row = lambda width: pl.BlockSpec((None, s, width), lambda b, i: (b, 0, 0))
    sds = jax.ShapeDtypeStruct
    return pl.pallas_call(
        functools.partial(_inproj_kernel, transposed_kv=False, n_alias=0),
        grid=(nb, 1),
        in_specs=[row(d), _const_spec((1, d)), _const_spec(w_rows.shape), _const_spec(w_ft.shape),
                  _const_spec((N_HEADS, d)), _const_spec((N_HEADS, 1))],
        out_specs=[row(ATT_WIDTH)] * 3 + [row(POOL_WIDTH), pl.BlockSpec((None, N_HEADS, s), lambda b, i: (b, 0, 0))],
        out_shape=[sds((nb, s, ATT_WIDTH), BF16), sds((nb, s, ATT_WIDTH), F32), sds((nb, s, ATT_WIDTH), F32),
                   sds((nb, s, POOL_WIDTH), F32), sds((nb, N_HEADS, s), F32)],
        compiler_params=_params(("arbitrary", "arbitrary")),
        name="inproj_rows",
    )(h, g, w_rows, w_ft, w_ft, b_f)


def _inproj_cols(h, g, w_qu, w_kvt, w_ft, b_f, layer, depth, kv_prev, *, tm):
    nb, s, d = h.shape
    row = lambda width: pl.BlockSpec((None, tm, width), lambda b, i: (b, i, 0))
    col = lambda height: pl.BlockSpec((None, height, tm), lambda b, i: (b, 0, i))
    if kv_prev is None:
        layer_col = lambda height: pl.BlockSpec((depth, None, height, tm), lambda b, i: (0, b, 0, i))
    else:
        layer_col = lambda height: pl.BlockSpec((None, None, height, tm), lambda b, i: (layer, b, 0, i))
    sds = jax.ShapeDtypeStruct
    out_shape = [sds((nb, s, ATT_WIDTH), BF16), sds((nb, s, POOL_WIDTH), F32),
                 sds((depth, nb, ATT_WIDTH, s), F32), sds((depth, nb, ATT_WIDTH, s), F32),
                 sds((nb, ATT_WIDTH, s), BF16), sds((nb, ATT_WIDTH, s), BF16),
                 sds((depth, nb, N_HEADS, s), F32), sds((nb, N_HEADS, s), F32)]
    out_specs = [row(ATT_WIDTH), row(POOL_WIDTH), layer_col(ATT_WIDTH), layer_col(ATT_WIDTH),
                 col(ATT_WIDTH), col(ATT_WIDTH), layer_col(N_HEADS), col(N_HEADS)]
    in_specs = [row(d), _const_spec((1, d)), _const_spec(w_qu.shape), _const_spec(w_kvt.shape),
                _const_spec((N_HEADS, d)), _const_spec((N_HEADS, 1))]
    args = [h, g, w_qu, w_kvt, w_ft, b_f]
    aliases = {}
    if kv_prev is not None:
        for out_idx, buf in zip((2, 3, 6), kv_prev):
            aliases[len(args)] = out_idx
            in_specs.append(pl.BlockSpec(memory_space=pl.ANY))
            args.append(buf)
    return pl.pallas_call(
        functools.partial(_inproj_kernel, transposed_kv=True, n_alias=len(aliases),
                          layer=layer if kv_prev is None else None),
        grid=(nb, s // tm),
        in_specs=in_specs,
        out_specs=out_specs,
        out_shape=out_shape,
        scratch_shapes=[pltpu.VMEM((N_HEADS, LANES), F32)],
        input_output_aliases=aliases,
        compiler_params=_params(("arbitrary", "arbitrary")),
        name="inproj_cols",
    )(*args)


def _prompt_attn_kernel(pi_ref, pj_ref, q_ref, kt_ref, vt_ref, c_ref, o_ref,
                        qs_ref, kta_ref, vta_ref, acc_ref, m_ref, s_ref, p_ref, alpha_ref, *, bq, bk):
    s_len = q_ref.shape[0]
    nq = s_len // bq
    per = bk // bq
    first_lanes = lax.broadcasted_iota(jnp.int32, (1, LANES), 1) < HEAD_DIM
    first_rows = lax.broadcasted_iota(jnp.int32, (LANES, 1), 0) < HEAD_DIM

    q2 = q_ref[...].reshape(nq, bq, LANES)
    zero = jnp.zeros_like(q2)
    lane = lax.broadcasted_iota(jnp.int32, (bq, LANES), 1)
    ones_a = jnp.where(lane < 3, 1.0, 0.0).astype(BF16)
    ones_b = jnp.where((lane >= 3) & (lane < 6), 1.0, 0.0).astype(BF16)
    first_q = (lane < HEAD_DIM)[None]
    qs_ref[:, :bq, :LANES] = jnp.where(first_q, q2, zero)
    qs_ref[:, bq:, :LANES] = jnp.where(first_q, zero, q2)

    @pl.when((pl.program_id(0) == 0) & (pl.program_id(1) == 0))
    def _():
        qs_ref[:, :bq, LANES:] = jnp.broadcast_to(ones_a[None], (nq, bq, LANES))
        qs_ref[:, bq:, LANES:] = jnp.broadcast_to(ones_b[None], (nq, bq, LANES))
        kta_ref[LANES + 16:, :] = jnp.zeros((LANES - 16, s_len), BF16)

    kta_ref[:LANES, :] = kt_ref[...]
    neg_c = -c_ref[...]
    parts = _split3(neg_c[0:1, :]) + _split3(neg_c[1:2, :])
    r16 = lax.broadcasted_iota(jnp.int32, (16, s_len), 0)
    extra = jnp.zeros((16, s_len), F32)
    for idx, part in enumerate(parts):
        extra = jnp.where(r16 == idx, part.astype(F32), extra)
    kta_ref[LANES:LANES + 16, :] = extra.astype(BF16)
    vt = vt_ref[...]
    one = jnp.ones_like(vt)
    vta_ref[0] = jnp.where(first_rows, vt, one)
    vta_ref[1] = jnp.where(first_rows, one, vt)

    def scores(i, j, slot, masked):
        cols = pl.ds(pl.multiple_of(j * bk, bk), bk)
        s2 = jnp.dot(qs_ref[i], kta_ref[:, cols], preferred_element_type=F32)
        if masked:
            r = (lax.broadcasted_iota(jnp.int32, s2.shape, 0) & (bq - 1)) + (i * bq - j * bk)
            c = lax.broadcasted_iota(jnp.int32, s2.shape, 1)
            s2 = jnp.where(c <= r, s2, NEG_INF)
        s_ref[slot] = s2

    def softmax(i, slot, first):
        rows = pl.ds(pl.multiple_of(i * bq, bq), bq)
        for hh in range(HEADS_PER_STEP):
            s = s_ref[slot, hh * bq:(hh + 1) * bq, :]
            m_new = jnp.broadcast_to(jnp.max(s, axis=1, keepdims=True), (bq, LANES))
            if not first:
                m_prev = m_ref[hh, rows, :]
                m_new = jnp.maximum(m_prev, m_new)
                alpha_ref[slot, hh] = jnp.exp(m_prev - m_new)
            p_ref[slot, hh] = jnp.concatenate(
                [jnp.exp(s[:, k * LANES:(k + 1) * LANES] - m_new) for k in range(bk // LANES)], axis=1).astype(BF16)
            m_ref[hh, rows, :] = m_new

    def accumulate(i, j, slot, first):
        rows = pl.ds(pl.multiple_of(i * bq, bq), bq)
        cols = pl.ds(pl.multiple_of(j * bk, bk), bk)
        for hh in range(HEADS_PER_STEP):
            pv = lax.dot_general(p_ref[slot, hh], vta_ref[hh, :, cols], NT_DIMS, preferred_element_type=F32)
            acc_ref[hh, rows, :] = pv if first else acc_ref[hh, rows, :] * alpha_ref[slot, hh] + pv

    def sweep(n, pair, masked):
        first = masked
        if n == 0:
            return
        scores(*pair(0), 0, masked)
        if n == 1:
            softmax(pair(0)[0], 0, first)
            accumulate(*pair(0), 0, first)
            return
        softmax(pair(0)[0], 0, first)
        scores(*pair(1), 1, masked)

        def tick(t, slot):
            accumulate(*pair(t - 2), slot, first)
            softmax(pair(t - 1)[0], 1 - slot, first)
            scores(*pair(t), slot, masked)

        per_trip = max([x for x in range(2, MAX_TICKS_PER_TRIP + 1, 2) if (n - 2) % x == 0], default=2)

        def ticks(k, carry):
            for x in range(per_trip):
                tick(per_trip * k + 2 + x, x % 2)
            return carry

        trips = (n - 2) // per_trip
        lax.fori_loop(0, trips, ticks, 0)
        for t in range(2 + trips * per_trip, n):
            tick(t, t % 2)
        accumulate(*pair(n - 2), n % 2, first)
        softmax(pair(n - 1)[0], (n - 1) % 2, first)
        accumulate(*pair(n - 1), (n - 1) % 2, first)

    def last_key_block(i):
        return i // per if isinstance(i, int) else lax.shift_right_logical(i, per.bit_length() - 1)

    sweep(nq, lambda t: (t, last_key_block(t)), True)
    sweep(sum(i // per for i in range(nq)), lambda t: (pi_ref[t], pj_ref[t]), False)

    def finish(i, carry):
        rows = pl.ds(pl.multiple_of(i * bq, bq), bq)
        a = acc_ref[0, rows, :]
        b = acc_ref[1, rows, :]
        sums = pltpu.roll(jnp.where(first_lanes, b, a), HEAD_DIM, axis=1)
        o_ref[rows, :] = (jnp.where(first_lanes, a, b) / sums).astype(o_ref.dtype)
        return carry

    lax.fori_loop(0, nq, finish, 0)


def _prompt_attn(q, ktb, vtb, ct, *, bq, bk):
    b, s, _ = q.shape
    nq = s // bq
    per = bk // bq
    assert s % bk == 0 and bk % bq == 0 and per & (per - 1) == 0 and bq & (bq - 1) == 0
    n_pairs = N_HEADS // HEADS_PER_STEP
    ct = ct.reshape(b, n_pairs, HEADS_PER_STEP, s)
    below = [(i, j) for i in range(nq) for j in range(i // per)] or [(0, 0)]
    pi = jnp.asarray(np.array([p[0] for p in below], np.int32))
    pj = jnp.asarray(np.array([p[1] for p in below], np.int32))
    qspec = pl.BlockSpec((None, s, LANES), lambda bi, hp, *_: (bi, 0, hp))
    kvspec = pl.BlockSpec((None, LANES, s), lambda bi, hp, *_: (bi, hp, 0))
    return pl.pallas_call(
        functools.partial(_prompt_attn_kernel, bq=bq, bk=bk),
        grid_spec=pltpu.PrefetchScalarGridSpec(
            num_scalar_prefetch=2,
            grid=(b, n_pairs),
            in_specs=[qspec, kvspec, kvspec,
                      pl.BlockSpec((None, None, HEADS_PER_STEP, s), lambda bi, hp, *_: (bi, hp, 0, 0))],
            out_specs=qspec,
            scratch_shapes=[pltpu.VMEM((nq, HEADS_PER_STEP * bq, MXU_DIM), BF16),
                            pltpu.VMEM((MXU_DIM, s), BF16),
                            pltpu.VMEM((HEADS_PER_STEP, LANES, s), BF16),
                            pltpu.VMEM((HEADS_PER_STEP, s, LANES), F32),
                            pltpu.VMEM((HEADS_PER_STEP, s, LANES), F32),
                            pltpu.VMEM((2, HEADS_PER_STEP * bq, bk), F32),
                            pltpu.VMEM((2, HEADS_PER_STEP, bq, bk), BF16),
                            pltpu.VMEM((2, HEADS_PER_STEP, bq, LANES), F32)],
        ),
        out_shape=jax.ShapeDtypeStruct((b, s, ATT_WIDTH), BF16),
        compiler_params=_params(("arbitrary", "arbitrary")),
        name="prompt_attn",
    )(pi, pj, q, ktb, vtb, ct)


def _decay_kernel(pt_ref, lf_hbm, wcat_ref, psuf_ref, o_ref, x_ref, sem, *, n_pool):
    d = pl.program_id(0)
    b = pl.program_id(1)
    nb = pl.num_programs(1)
    step = d * nb + b
    slot = step & 1
    n_pages = x_ref.shape[2]

    def page_copy(p, page, slot_):
        return pltpu.make_async_copy(lf_hbm.at[page], x_ref.at[slot_, :, pl.ds(p, 1), :], sem.at[slot_])

    def fetch(d_, b_, slot_):
        def start(p, carry):
            page_copy(p, pt_ref[b_, p] + d_ * n_pool, slot_).start()
            return carry
        lax.fori_loop(0, n_pages, start, 0, unroll=8)

    @pl.when(step == 0)
    def _():
        fetch(d, b, slot)

    @pl.when(step + 1 < pl.num_programs(0) * nb)
    def _():
        wrap = b + 1 == nb
        fetch(jnp.where(wrap, d + 1, d), jnp.where(wrap, 0, b + 1), 1 - slot)

    def wait(p, carry):
        page_copy(p, 0, slot).wait()
        return carry

    lax.fori_loop(0, n_pages, wait, 0, unroll=8)

    page = x_ref.shape[3]
    xs = _split3(x_ref[slot].reshape(N_HEADS * n_pages, page))
    wc = sum(jnp.dot(t, wcat_ref[...], preferred_element_type=F32) for t in xs)
    head = lambda h: slice(h * n_pages, (h + 1) * n_pages)
    totals = jnp.concatenate([wc[head(h), page:] for h in range(N_HEADS)], axis=1)
    later = sum(jnp.dot(psuf_ref[...], t, preferred_element_type=F32) for t in _split3(totals))
    for h in range(N_HEADS):
        o_ref[h] = wc[head(h), :page] + later[:, h * page:(h + 1) * page]


def _decay_bias(page_table, logf_pages):
    depth, n_pool, n_heads, page = logf_pages.shape
    db, n_pages = page_table.shape
    t = jnp.arange(page)
    tsuf = (t[:, None] > t[None, :]).astype(BF16)
    wcat = jnp.concatenate([tsuf, jnp.ones((page, page), BF16)], axis=1)
    pidx = jnp.arange(n_pages)
    psuf = (pidx[None, :] > pidx[:, None]).astype(BF16)
    const = lambda shape: pl.BlockSpec(shape, lambda d, b, pt: (0,) * len(shape), pipeline_mode=pl.Buffered(1))
    return pl.pallas_call(
        functools.partial(_decay_kernel, n_pool=n_pool),
        grid_spec=pltpu.PrefetchScalarGridSpec(
            num_scalar_prefetch=1,
            grid=(depth, db),
            in_specs=[pl.BlockSpec(memory_space=pl.ANY), const(wcat.shape), const(psuf.shape)],
            out_specs=pl.BlockSpec((None, None, n_heads, n_pages, page), lambda d, b, pt: (d, b, 0, 0, 0)),
            scratch_shapes=[pltpu.VMEM((2, n_heads, n_pages, page), F32), pltpu.SemaphoreType.DMA((2,))],
        ),
        out_shape=jax.ShapeDtypeStruct((depth, db, n_heads, n_pages, page), F32),
        compiler_params=_params(("arbitrary", "arbitrary")),
        name="decay_bias",
    )(page_table, logf_pages.reshape(depth * n_pool, n_heads, 1, page), wcat, psuf)


def _expand_heads(row_of_head, reps):
    return jnp.concatenate([jnp.broadcast_to(row_of_head(h), (reps, row_of_head(h).shape[1]))
                            for h in range(N_HEADS)], axis=0)


def _sample_attn_kernel(pt_ref, q_ref, kt_hbm, vt_hbm, bias_ref, kn_ref, vn_ref, lfn_ref, o_ref,
                        kbuf_ref, vbuf_ref, sem, qbd_ref, acc_ref, m_ref, l_ref, s_ref, kpad_ref, vpad_ref,
                        *, pps, t_new, base, lookahead):
    b = pl.program_id(0)
    j = pl.program_id(1)
    nb = pl.num_programs(0)
    n_steps = pl.num_programs(1) - 1
    n_slots = kbuf_ref.shape[0]
    chunk = b * n_steps + jnp.minimum(j, n_steps - 1)
    slot_of = lambda c: lax.rem(c, n_slots)

    def chunk_copies(bb, jj, slot, known_pages):
        copies = []
        for r in range(pps):
            pg = pt_ref[bb, jj * pps + r] + base if known_pages else 0
            copies.append(pltpu.make_async_copy(kt_hbm.at[pg], kbuf_ref.at[slot, r], sem.at[slot]))
            copies.append(pltpu.make_async_copy(vt_hbm.at[pg], vbuf_ref.at[slot, r], sem.at[slot]))
        return copies

    @pl.when((b == 0) & (j == 0))
    def _():
        for x in range(lookahead):
            for c in chunk_copies(0, x, x, True):
                c.start()

    @pl.when(j < n_steps)
    def _():
        ahead = j + lookahead
        wrap = ahead >= n_steps
        b_ahead = jnp.where(wrap, b + 1, b)
        j_ahead = jnp.where(wrap, ahead - n_steps, ahead)

        @pl.when(b_ahead < nb)
        def _():
            for c in chunk_copies(b_ahead, j_ahead, slot_of(chunk + lookahead), True):
                c.start()

        for c in chunk_copies(b, j, slot_of(chunk), False):
            c.wait()

    slot = slot_of(chunk)
    vslot = jnp.where((j == 0) | (j == n_steps), slot, slot_of(chunk + n_slots - 1))

    @pl.when(j == 0)
    def _():
        qt = jnp.concatenate([q_ref[...].astype(F32)] * N_HEADS, axis=0)
        r = lax.broadcasted_iota(jnp.int32, qt.shape, 0) // t_new
        c = lax.broadcasted_iota(jnp.int32, qt.shape, 1) // HEAD_DIM
        qbd_ref[...] = jnp.where(r == c, qt, 0.0).astype(BF16)
        acc_ref[...] = jnp.zeros_like(acc_ref)
        l_ref[...] = jnp.zeros_like(l_ref)
        m_ref[...] = jnp.full_like(m_ref, NEG_INF)
        s_ref[...] = jnp.full_like(s_ref, NEG_INF)

    def update(scores, weighted_values):
        m_prev = m_ref[...]
        m_cur = scores[0].max(axis=1, keepdims=True)
        for s in scores[1:]:
            m_cur = jnp.maximum(m_cur, s.max(axis=1, keepdims=True))
        m_new = jnp.maximum(m_prev, m_cur)
        alpha = jnp.exp(m_prev - m_new)
        l_new = alpha * l_ref[...]
        acc = alpha[:, :1] * acc_ref[...]
        for s, pv in zip(scores, weighted_values):
            p = jnp.exp(s - m_new[:, :1])
            l_new = l_new + p.sum(axis=1, keepdims=True)
            acc = acc + pv(p.astype(BF16))
        m_ref[...] = m_new
        l_ref[...] = l_new
        acc_ref[...] = acc

    waiting = [s_ref[r] for r in range(pps)]
    qbd = qbd_ref[...]
    for r in range(pps):
        s = jnp.dot(qbd, kbuf_ref[slot, r].astype(BF16), preferred_element_type=F32)
        s_ref[r] = s + _expand_heads(lambda h: bias_ref[h, r:r + 1, :], t_new)
    update(waiting,
           [lambda p, r=r: lax.dot_general(p, vbuf_ref[vslot, r].astype(BF16), NT_DIMS, preferred_element_type=F32)
            for r in range(pps)])

    @pl.when(j == pl.num_programs(1) - 1)
    def _():
        kpad_ref[...] = jnp.zeros_like(kpad_ref)
        vpad_ref[...] = jnp.zeros_like(vpad_ref)
        kpad_ref[:t_new, :] = kn_ref[...]
        vpad_ref[:t_new, :] = vn_ref[...]
        cn = _lane_cumsum(lfn_ref[...])
        s = lax.dot_general(qbd, kpad_ref[...].astype(BF16), NT_DIMS, preferred_element_type=F32)
        s = s - _expand_heads(lambda h: cn[h:h + 1, :], t_new)
        qi = lax.broadcasted_iota(jnp.int32, s.shape, 0) % t_new
        ki = lax.broadcasted_iota(jnp.int32, s.shape, 1)
        s = jnp.where(ki <= qi, s, NEG_INF)
        update([s], [lambda p: jnp.dot(p, vpad_ref[...].astype(BF16), preferred_element_type=F32)])
        out = acc_ref[...] / l_ref[:, :1]
        res = jnp.zeros((t_new, ATT_WIDTH), F32)
        lane_head = lax.broadcasted_iota(jnp.int32, res.shape, 1) // HEAD_DIM
        for h in range(N_HEADS):
            res = jnp.where(lane_head == h, out[h * t_new:(h + 1) * t_new, :], res)
        o_ref[...] = res.astype(o_ref.dtype)


def _sample_attn(page_table, q, k_new, v_new, lf_new, kt_pages, vt_pages, bias, layer, *, pps):
    db, t_new, _ = q.shape
    n_pages = page_table.shape[1]
    page = kt_pages.shape[2]
    n_pool = kt_pages.shape[0] // bias.shape[0]
    base = layer * n_pool
    rows = N_HEADS * t_new
    n_steps = n_pages // pps
    lookahead = min(2, n_steps)
    n_slots = lookahead + 2
    key_step = lambda j: jnp.minimum(j, n_steps - 1)
    tok = pl.BlockSpec((None, t_new, ATT_WIDTH), lambda b, j, pt: (b, 0, 0))
    hbm = pl.BlockSpec(memory_space=pl.ANY)
    in_specs = [tok, hbm, hbm,
                pl.BlockSpec((None, None, N_HEADS, pps, page), lambda b, j, pt: (layer, b, 0, key_step(j), 0)),
                tok, tok, pl.BlockSpec((None, N_HEADS, LANES), lambda b, j, pt: (b, 0, 0))]
    return pl.pallas_call(
        functools.partial(_sample_attn_kernel, pps=pps, t_new=t_new, base=base, lookahead=lookahead),
        grid_spec=pltpu.PrefetchScalarGridSpec(
            num_scalar_prefetch=1,
            grid=(db, n_steps + 1),
            in_specs=in_specs,
            out_specs=tok,
            scratch_shapes=[pltpu.VMEM((n_slots, pps, ATT_WIDTH, page), F32),
                            pltpu.VMEM((n_slots, pps, ATT_WIDTH, page), F32),
                            pltpu.SemaphoreType.DMA((n_slots,)),
                            pltpu.VMEM((rows, ATT_WIDTH), BF16), pltpu.VMEM((rows, ATT_WIDTH), F32),
                            pltpu.VMEM((rows, LANES), F32), pltpu.VMEM((rows, LANES), F32),
                            pltpu.VMEM((pps, rows, page), F32),
                            pltpu.VMEM((page, ATT_WIDTH), F32), pltpu.VMEM((page, ATT_WIDTH), F32)],
        ),
        out_shape=jax.ShapeDtypeStruct((db, t_new, ATT_WIDTH), BF16),
        compiler_params=_params(("arbitrary", "arbitrary")),
        name="sample_attn",
    )(page_table, q, kt_pages, vt_pages, bias, k_new, v_new, lf_new)


def _pool_kernel(*refs, pos0, tm, use_halo):
    if use_halo:
        u_ref, halo_ref, hist_ref, o_ref, z_ref = refs
        prev = jnp.where(pl.program_id(1) == 0, hist_ref[...], halo_ref[...])
    else:
        u_ref, hist_ref, o_ref, z_ref = refs
        prev = hist_ref[...]
    z_ref[:HIST_ROWS, :] = prev
    z_ref[HIST_ROWS:, :] = u_ref[...]
    row = lax.broadcasted_iota(jnp.int32, (tm, 1), 0)
    pos_count = pos0 + pl.program_id(1) * tm + row + 1
    for g, w in enumerate(POOL_WINDOWS):
        cols = slice(g * POOL_GROUP, (g + 1) * POOL_GROUP)
        total = z_ref[:, cols]
        shift = 1
        while shift < w:
            total = total + pltpu.roll(total, shift, axis=0)
            shift *= 2
        cnt = jnp.minimum(pos_count, w).astype(F32)
        o_ref[:, cols] = (total[HIST_ROWS:, :] / cnt - z_ref[HIST_ROWS:, cols]).astype(o_ref.dtype)


def _pool_means(u, hist, *, pos0, tm):
    nb, t, _ = u.shape
    use_halo = t > tm
    row = pl.BlockSpec((None, tm, POOL_WIDTH), lambda b, i: (b, i, 0))
    hist_spec = pl.BlockSpec((None, HIST_ROWS, POOL_WIDTH), lambda b, i: (b, 0, 0))
    per = tm // HIST_ROWS
    halo = pl.BlockSpec((None, HIST_ROWS, POOL_WIDTH), lambda b, i: (b, jnp.maximum(i * per - 1, 0), 0))
    in_specs = [row, halo, hist_spec] if use_halo else [row, hist_spec]
    args = (u, u, hist) if use_halo else (u, hist)
    return pl.pallas_call(
        functools.partial(_pool_kernel, pos0=pos0, tm=tm, use_halo=use_halo),
        grid=(nb, t // tm),
        in_specs=in_specs,
        out_specs=row,
        out_shape=jax.ShapeDtypeStruct(u.shape, BF16),
        scratch_shapes=[pltpu.VMEM((HIST_ROWS + tm, POOL_WIDTH), F32)],
        compiler_params=_params(("arbitrary", "arbitrary")),
        name="pool_means",
    )(*args)


def _post_kernel(h_ref, att_ref, pooled_ref, p_ref, wpool_ref, pscale_ref, gfox_ref, gpool_ref, wout_ref,
                 gpostmix_ref, gpremlp_ref, wup_ref, wdown_ref, gpostmlp_ref, wple_ref, gple_ref, wgate_ref,
                 o_ref, acc_ref, *, ff_chunk):
    dot = functools.partial(jnp.dot, preferred_element_type=F32)
    an = _rms(att_ref[...].astype(F32), gfox_ref[...]).astype(BF16)
    pooled = pooled_ref[...]
    ys = [dot(pooled[:, g * POOL_GROUP:(g + 1) * POOL_GROUP], wpool_ref[g]) for g in range(len(POOL_WINDOWS))]
    y = jnp.concatenate(ys, axis=-1) * pscale_ref[...]
    pn = _rms(y, gpool_ref[...]).astype(BF16)
    mix = dot(an, wout_ref[:ATT_WIDTH, :]) + dot(pn, wout_ref[ATT_WIDTH:, :])
    h1 = h_ref[...] + _rms(mix, gpostmix_ref[...])
    m = _rms(h1, gpremlp_ref[...]).astype(BF16)
    d_ff = wup_ref.shape[1]
    for c in range(d_ff // ff_chunk):
        cols = slice(c * ff_chunk, (c + 1) * ff_chunk)
        hidden = jnp.square(jnp.maximum(dot(m, wup_ref[:, cols]), 0.0)).astype(BF16)
        part = dot(hidden, wdown_ref[cols, :])
        if c == 0:
            acc_ref[...] = part
        else:
            acc_ref[...] += part
    h2 = h1 + _rms(acc_ref[...], gpostmlp_ref[...])
    e = _rms(dot(p_ref[...].astype(BF16), wple_ref[...]), gple_ref[...])
    gate = jax.nn.sigmoid(dot(h2.astype(BF16), wgate_ref[...]))
    o_ref[...] = h2 + gate * e


def _post(h, att, pooled, p, w, *, tm, ff_chunk):
    n, d = h.shape
    row = lambda width: pl.BlockSpec((tm, width), lambda i: (i, 0))
    weights = (w["w_pool"], w["pool_scale"], w["g_fox"], w["g_pool"], w["w_out"], w["g_post_mix"], w["g_pre_mlp"],
               w["w_up"], w["w_down"], w["g_post_mlp"], w["w_ple"], w["g_ple"], w["w_ple_gate"])
    return pl.pallas_call(
        functools.partial(_post_kernel, ff_chunk=ff_chunk),
        grid=(n // tm,),
        in_specs=[row(d), row(ATT_WIDTH), row(POOL_WIDTH), row(p.shape[1])] + [_const_spec(x.shape) for x in weights],
        out_specs=row(d),
        out_shape=jax.ShapeDtypeStruct((n, d), F32),
        scratch_shapes=[pltpu.VMEM((tm, d), F32)],
        compiler_params=_params(("arbitrary",)),
        name="post",
    )(h, att, pooled, p, *weights)


def _tile(n, target):
    t = min(n, target)
    assert n % t == 0, (n, t)
    return t


def kernel(x_prompt, x_sample, cache_k, cache_v, cache_logf, state_pool, page_table, p_prompt, p_sample, w_in, b_f, w_pool, pool_scale, g_fox, g_pool, w_out, g_pre_mix, g_post_mix, g_pre_mlp, g_post_mlp, w_up, w_down, w_ple, g_ple, w_ple_gate):
    b, s, d = x_prompt.shape
    db, t_new, _ = x_sample.shape
    depth, n_pool, page, n_heads, head_dim = cache_k.shape
    n_pages = page_table.shape[1]
    assert (n_heads, head_dim) == (N_HEADS, HEAD_DIM) and w_in.shape[2] == 3 * ATT_WIDTH + POOL_WIDTH + N_HEADS
    assert page == LANES and t_new == 8 and state_pool.shape[2:] == (POOL_BUF, POOL_WIDTH)
    w = ATT_WIDTH
    n_main = 3 * w + POOL_WIDTH
    past = n_pages * page

    w_bf = w_in.astype(BF16)
    w_rows = w_bf[:, :, :n_main]
    w_qu = jnp.concatenate([w_bf[:, :, :w], w_bf[:, :, 3 * w:n_main]], axis=2)
    w_t = jnp.swapaxes(w_bf, 1, 2)
    w_kvt = w_t[:, w:3 * w]
    w_ft = w_t[:, n_main:]
    vec = lambda x: x[:, None, :]
    layers = []
    for i in range(depth):
        layers.append(dict(
            w_pool=w_pool[i].astype(BF16), pool_scale=vec(pool_scale)[i], g_fox=vec(g_fox)[i], g_pool=vec(g_pool)[i],
            w_out=w_out[i].astype(BF16), g_post_mix=vec(g_post_mix)[i], g_pre_mlp=vec(g_pre_mlp)[i],
            w_up=w_up[i].astype(BF16), w_down=w_down[i].astype(BF16), g_post_mlp=vec(g_post_mlp)[i],
            w_ple=w_ple[i].astype(BF16), g_ple=vec(g_ple)[i], w_ple_gate=w_ple_gate[i].astype(BF16)))

    kt_pages = jnp.transpose(cache_k, (0, 1, 3, 4, 2)).reshape(depth * n_pool, w, page)
    vt_pages = jnp.transpose(cache_v, (0, 1, 3, 4, 2)).reshape(depth * n_pool, w, page)
    bias = _decay_bias(page_table, jnp.swapaxes(cache_logf, 2, 3))
    zero_hist = jnp.zeros((b, HIST_ROWS, POOL_WIDTH), F32)

    tm_in = _tile(s, 512)
    bq = _tile(s, MXU_DIM)
    bk = _tile(s, 2 * MXU_DIM)
    tm_pool = _tile(s, 512)
    tm_post = _tile(b * s, 512)
    pps = _tile(n_pages, 8)
    n_s = db * t_new
    tok3 = lambda x: x.reshape(db, t_new, x.shape[-1])

    hp = x_prompt
    hs = x_sample.reshape(1, n_s, d)
    kv_bufs = None
    pool_p, k_s, v_s, lf_s_all, pool_s = [], [], [], [], []
    for i in range(depth):
        g_pre = vec(g_pre_mix)[i]
        bf_col = b_f[i][:, None]
        q, u, kt_all, vt_all, ktb, vtb, lft_all, ct = _inproj_cols(hp, g_pre, w_qu[i], w_kvt[i], w_ft[i], bf_col, i, depth,
                                                                   kv_bufs, tm=tm_in)
        kv_bufs = (kt_all, vt_all, lft_all)
        att = _prompt_attn(q, ktb, vtb, ct, bq=bq, bk=bk)
        pooled = _pool_means(u, zero_hist, pos0=0, tm=tm_pool)
        hp = _post(hp.reshape(b * s, d), att.reshape(b * s, w), pooled.reshape(b * s, POOL_WIDTH),
                   p_prompt[i].reshape(b * s, -1), layers[i], tm=tm_post, ff_chunk=512).reshape(b, s, d)
        pool_p.append(u[:, s - POOL_BUF:, :])
        qs, ks, vs, us, lfts = _inproj_rows(hs, g_pre, w_rows[i], w_ft[i], bf_col)
        lf_s = jnp.swapaxes(lfts[0], 0, 1).reshape(db, t_new, N_HEADS)
        lf_new = jnp.pad(jnp.swapaxes(lf_s, 1, 2), ((0, 0), (0, 0), (0, LANES - t_new)))
        att_s = _sample_attn(page_table, tok3(qs[0]), tok3(ks[0]), tok3(vs[0]), lf_new, kt_pages, vt_pages, bias, i,
                             pps=pps)
        hist = jnp.pad(state_pool[i], ((0, 0), (1, 0), (0, 0)))
        pooled_s = _pool_means(tok3(us[0]), hist, pos0=past, tm=t_new)
        hs = _post(hs.reshape(n_s, d), att_s.reshape(n_s, w), pooled_s.reshape(n_s, POOL_WIDTH),
                   p_sample[i].reshape(n_s, -1), layers[i], tm=n_s, ff_chunk=512).reshape(1, n_s, d)
        k_s.append(ks.reshape(db, t_new, N_HEADS, HEAD_DIM))
        v_s.append(vs.reshape(db, t_new, N_HEADS, HEAD_DIM))
        lf_s_all.append(lf_s)
        pool_s.append(jnp.concatenate([state_pool[i], tok3(us[0])], axis=1)[:, -POOL_BUF:])
    kt_all, vt_all, lft_all = kv_bufs
    heads_last = lambda xt: jnp.transpose(xt.reshape(depth, b, N_HEADS, HEAD_DIM, s), (0, 1, 4, 2, 3))
    return (hp, hs.reshape(db, t_new, d), heads_last(kt_all), heads_last(vt_all), jnp.swapaxes(lft_all, 2, 3),
            jnp.stack(pool_p), jnp.stack(k_s), jnp.stack(v_s), jnp.stack(lf_s_all), jnp.stack(pool_s))
```

```python
import functools
import math

import jax
import jax.numpy as jnp
import numpy as np
from jax import lax
from jax.experimental import pallas as pl
from jax.experimental.pallas import tpu as pltpu

F32 = jnp.float32
BF16 = jnp.bfloat16

EPS = 1e-6
NEG_INF = -1e30
HEAD_DIM = 64
N_HEADS = 8
ATT_WIDTH = N_HEADS * HEAD_DIM
POOL_WINDOWS = (2, 4, 8, 16)
POOL_GROUP = 128
POOL_WIDTH = POOL_GROUP * len(POOL_WINDOWS)
POOL_BUF = max(POOL_WINDOWS) - 1
HIST_ROWS = POOL_BUF + 1
ATT_SCALE = 1.0 / math.sqrt(HEAD_DIM)

LANES = 128
MXU_DIM = 256
HEADS_PER_STEP = LANES // HEAD_DIM
MAX_TICKS_PER_TRIP = 18
VMEM_LIMIT = 56 * 1024 * 1024

NT_DIMS = (((1,), (1,)), ((), ()))


def _params(semantics):
    return pltpu.CompilerParams(dimension_semantics=semantics, vmem_limit_bytes=VMEM_LIMIT)


def _rms(x, g):
    return x * lax.rsqrt(jnp.mean(x * x, axis=-1, keepdims=True) + EPS) * g


def _const_spec(shape):
    zeros = (0,) * len(shape)
    return pl.BlockSpec(shape, lambda *_: zeros, pipeline_mode=pl.Buffered(1))


def _lane_cumsum(x):
    n = x.shape[1]
    lane = lax.broadcasted_iota(jnp.int32, x.shape, 1)
    shift = 1
    while shift < n:
        x = x + jnp.where(lane >= shift, pltpu.roll(x, shift, axis=1), 0.0)
        shift *= 2
    return x


def _split3(x):
    x1 = x.astype(BF16)
    r1 = x - x1.astype(F32)
    x2 = r1.astype(BF16)
    x3 = (r1 - x2.astype(F32)).astype(BF16)
    return x1, x2, x3


def _inproj_kernel(h_ref, g_ref, w_ref, wt_ref, wft_ref, bf_ref, *refs, transposed_kv, n_alias=0, layer=None):
    refs = refs[n_alias:]

    def put(ref, val):
        if layer is None:
            ref[...] = val
        else:
            for l in range(ref.shape[0]):
                ref[l] = val if l == layer else jnp.zeros_like(val)

    if transposed_kv:
        @pl.when(pl.program_id(1) == 0)
        def _():
            refs[-2][...] = jnp.zeros_like(refs[-2])
            refs[-1][:HIST_ROWS, :] = jnp.zeros((HIST_ROWS, POOL_WIDTH), F32)

    a = _rms(h_ref[...], g_ref[...]).astype(BF16)
    w = ATT_WIDTH
    f = lax.dot_general(wft_ref[...], a, NT_DIMS, preferred_element_type=F32) + bf_ref[...]
    lf = jnp.minimum(f, 0.0) - jnp.log1p(jnp.exp(-jnp.abs(f)))
    if transposed_kv:
        q_ref, pooled_ref, utail_ref, kt_ref, vt_ref, ktb_ref, vtb_ref, lft_ref, ct_ref, carry_ref, z_ref = refs
        tm = q_ref.shape[0]
        put(lft_ref, lf)
        c = _lane_cumsum(lf) + carry_ref[:, :1]
        ct_ref[...] = c
        carry_ref[...] = jnp.broadcast_to(c[:, -1:], carry_ref.shape)
        z = jnp.dot(a, w_ref[...], preferred_element_type=F32)
        q_ref[...] = (z[:, :w] * ATT_SCALE).astype(BF16)
        z_ref[HIST_ROWS:, :] = z[:, w:]
        _pooled_rows(z_ref, pooled_ref, pl.program_id(1) * tm)
        tail = z_ref[tm:, :]
        utail_ref[...] = tail
        z_ref[:HIST_ROWS, :] = tail
        zt = lax.dot_general(wt_ref[...], a, NT_DIMS, preferred_element_type=F32)
        put(kt_ref, zt[:w])
        put(vt_ref, zt[w:])
        ktb_ref[...] = zt[:w].astype(BF16)
        vtb_ref[...] = zt[w:].astype(BF16)
    else:
        z = jnp.dot(a, w_ref[...], preferred_element_type=F32)
        q_ref, k_ref, v_ref, u_ref, lft_ref = refs
        q_ref[...] = (z[:, :w] * ATT_SCALE).astype(BF16)
        k_ref[...] = z[:, w:2 * w]
        v_ref[...] = z[:, 2 * w:3 * w]
        u_ref[...] = z[:, 3 * w:]
        lft_ref[...] = lf


def _inproj_rows(h, g, w_rows, w_ft, b_f):
    nb, s, d = h.shape
    row = lambda width: pl.BlockSpec((None, s, width), lambda b, i: (b, 0, 0))
    sds = jax.ShapeDtypeStruct
    return pl.pallas_call(
        functools.partial(_inproj_kernel, transposed_kv=False, n_alias=0),
        grid=(nb, 1),
        in_specs=[row(d), _const_spec((1, d)), _const_spec(w_rows.shape), _const_spec(w_ft.shape),
                  _const_spec((N_HEADS, d)), _const_spec((N_HEADS, 1))],
        out_specs=[row(ATT_WIDTH)] * 3 + [row(POOL_WIDTH), pl.BlockSpec((None, N_HEADS, s), lambda b, i: (b, 0, 0))],
        out_shape=[sds((nb, s, ATT_WIDTH), BF16), sds((nb, s, ATT_WIDTH), F32), sds((nb, s, ATT_WIDTH), F32),
                   sds((nb, s, POOL_WIDTH), F32), sds((nb, N_HEADS, s), F32)],
        compiler_params=_params(("arbitrary", "arbitrary")),
        name="inproj_rows",
    )(h, g, w_rows, w_ft, w_ft, b_f)


def _inproj_cols(h, g, w_qu, w_kvt, w_ft, b_f, layer, depth, kv_prev, *, tm):
    nb, s, d = h.shape
    row = lambda width: pl.BlockSpec((None, tm, width), lambda b, i: (b, i, 0))
    col = lambda height: pl.BlockSpec((None, height, tm), lambda b, i: (b, 0, i))
    if kv_prev is None:
        layer_col = lambda height: pl.BlockSpec((depth, None, height, tm), lambda b, i: (0, b, 0, i))
    else:
        layer_col = lambda height: pl.BlockSpec((None, None, height, tm), lambda b, i: (layer, b, 0, i))
    sds = jax.ShapeDtypeStruct
    out_shape = [sds((nb, s, ATT_WIDTH), BF16), sds((nb, s, POOL_WIDTH), BF16), sds((nb, HIST_ROWS, POOL_WIDTH), F32),
                 sds((depth, nb, ATT_WIDTH, s), F32), sds((depth, nb, ATT_WIDTH, s), F32),
                 sds((nb, ATT_WIDTH, s), BF16), sds((nb, ATT_WIDTH, s), BF16),
                 sds((depth, nb, N_HEADS, s), F32), sds((nb, N_HEADS, s), F32)]
    out_specs = [row(ATT_WIDTH), row(POOL_WIDTH), pl.BlockSpec((None, HIST_ROWS, POOL_WIDTH), lambda b, i: (b, 0, 0)),
                 layer_col(ATT_WIDTH), layer_col(ATT_WIDTH),
                 col(ATT_WIDTH), col(ATT_WIDTH), layer_col(N_HEADS), col(N_HEADS)]
    in_specs = [row(d), _const_spec((1, d)), _const_spec(w_qu.shape), _const_spec(w_kvt.shape),
                _const_spec((N_HEADS, d)), _const_spec((N_HEADS, 1))]
    args = [h, g, w_qu, w_kvt, w_ft, b_f]
    aliases = {}
    if kv_prev is not None:
        for out_idx, buf in zip((3, 4, 7), kv_prev):
            aliases[len(args)] = out_idx
            in_specs.append(pl.BlockSpec(memory_space=pl.ANY))
            args.append(buf)
    return pl.pallas_call(
        functools.partial(_inproj_kernel, transposed_kv=True, n_alias=len(aliases),
                          layer=layer if kv_prev is None else None),
        grid=(nb, s // tm),
        in_specs=in_specs,
        out_specs=out_specs,
        out_shape=out_shape,
        scratch_shapes=[pltpu.VMEM((N_HEADS, LANES), F32), pltpu.VMEM((HIST_ROWS + tm, POOL_WIDTH), F32)],
        input_output_aliases=aliases,
        compiler_params=_params(("arbitrary", "arbitrary")),
        name="inproj_cols",
    )(*args)


def _prompt_attn_kernel(pi_ref, pj_ref, q_ref, kt_ref, vt_ref, c_ref, o_ref,
                        qs_ref, kta_ref, vta_ref, acc_ref, m_ref, s_ref, p_ref, alpha_ref, *, bq, bk):
    s_len = q_ref.shape[0]
    nq = s_len // bq
    per = bk // bq
    first_lanes = lax.broadcasted_iota(jnp.int32, (1, LANES), 1) < HEAD_DIM
    first_rows = lax.broadcasted_iota(jnp.int32, (LANES, 1), 0) < HEAD_DIM

    q2 = q_ref[...].reshape(nq, bq, LANES)
    zero = jnp.zeros_like(q2)
    lane = lax.broadcasted_iota(jnp.int32, (bq, LANES), 1)
    ones_a = jnp.where(lane < 3, 1.0, 0.0).astype(BF16)
    ones_b = jnp.where((lane >= 3) & (lane < 6), 1.0, 0.0).astype(BF16)
    first_q = (lane < HEAD_DIM)[None]
    qs_ref[:, :bq, :LANES] = jnp.where(first_q, q2, zero)
    qs_ref[:, bq:, :LANES] = jnp.where(first_q, zero, q2)

    @pl.when((pl.program_id(0) == 0) & (pl.program_id(1) == 0))
    def _():
        qs_ref[:, :bq, LANES:] = jnp.broadcast_to(ones_a[None], (nq, bq, LANES))
        qs_ref[:, bq:, LANES:] = jnp.broadcast_to(ones_b[None], (nq, bq, LANES))
        kta_ref[LANES + 16:, :] = jnp.zeros((LANES - 16, s_len), BF16)

    kta_ref[:LANES, :] = kt_ref[...]
    neg_c = -c_ref[...]
    parts = _split3(neg_c[0:1, :]) + _split3(neg_c[1:2, :])
    r16 = lax.broadcasted_iota(jnp.int32, (16, s_len), 0)
    extra = jnp.zeros((16, s_len), F32)
    for idx, part in enumerate(parts):
        extra = jnp.where(r16 == idx, part.astype(F32), extra)
    kta_ref[LANES:LANES + 16, :] = extra.astype(BF16)
    vt = vt_ref[...]
    one = jnp.ones_like(vt)
    vta_ref[0] = jnp.where(first_rows, vt, one)
    vta_ref[1] = jnp.where(first_rows, one, vt)

    def scores(i, j, slot, masked):
        cols = pl.ds(pl.multiple_of(j * bk, bk), bk)
        s2 = jnp.dot(qs_ref[i], kta_ref[:, cols], preferred_element_type=F32)
        if masked:
            r = (lax.broadcasted_iota(jnp.int32, s2.shape, 0) & (bq - 1)) + (i * bq - j * bk)
            c = lax.broadcasted_iota(jnp.int32, s2.shape, 1)
            s2 = jnp.where(c <= r, s2, NEG_INF)
        s_ref[slot] = s2

    def softmax(i, slot, first):
        rows = pl.ds(pl.multiple_of(i * bq, bq), bq)
        for hh in range(HEADS_PER_STEP):
            s = s_ref[slot, hh * bq:(hh + 1) * bq, :]
            m_new = jnp.broadcast_to(jnp.max(s, axis=1, keepdims=True), (bq, LANES))
            if not first:
                m_prev = m_ref[hh, rows, :]
                m_new = jnp.maximum(m_prev, m_new)
                alpha_ref[slot, hh] = jnp.exp(m_prev - m_new)
            p_ref[slot, hh] = jnp.concatenate(
                [jnp.exp(s[:, k * LANES:(k + 1) * LANES] - m_new) for k in range(bk // LANES)], axis=1).astype(BF16)
            m_ref[hh, rows, :] = m_new

    def accumulate(i, j, slot, first):
        rows = pl.ds(pl.multiple_of(i * bq, bq), bq)
        cols = pl.ds(pl.multiple_of(j * bk, bk), bk)
        for hh in range(HEADS_PER_STEP):
            pv = lax.dot_general(p_ref[slot, hh], vta_ref[hh, :, cols], NT_DIMS, preferred_element_type=F32)
            acc_ref[hh, rows, :] = pv if first else acc_ref[hh, rows, :] * alpha_ref[slot, hh] + pv

    def sweep(n, pair, masked):
        first = masked
        if n == 0:
            return
        scores(*pair(0), 0, masked)
        if n == 1:
            softmax(pair(0)[0], 0, first)
            accumulate(*pair(0), 0, first)
            return
        softmax(pair(0)[0], 0, first)
        scores(*pair(1), 1, masked)

        def tick(t, slot):
            accumulate(*pair(t - 2), slot, first)
            softmax(pair(t - 1)[0], 1 - slot, first)
            scores(*pair(t), slot, masked)

        per_trip = max([x for x in range(2, MAX_TICKS_PER_TRIP + 1, 2) if (n - 2) % x == 0], default=2)

        def ticks(k, carry):
            for x in range(per_trip):
                tick(per_trip * k + 2 + x, x % 2)
            return carry

        trips = (n - 2) // per_trip
        lax.fori_loop(0, trips, ticks, 0)
        for t in range(2 + trips * per_trip, n):
            tick(t, t % 2)
        accumulate(*pair(n - 2), n % 2, first)
        softmax(pair(n - 1)[0], (n - 1) % 2, first)
        accumulate(*pair(n - 1), (n - 1) % 2, first)

    def last_key_block(i):
        return i // per if isinstance(i, int) else lax.shift_right_logical(i, per.bit_length() - 1)

    sweep(nq, lambda t: (t, last_key_block(t)), True)
    sweep(sum(i // per for i in range(nq)), lambda t: (pi_ref[t], pj_ref[t]), False)

    def finish(i, carry):
        rows = pl.ds(pl.multiple_of(i * bq, bq), bq)
        a = acc_ref[0, rows, :]
        b = acc_ref[1, rows, :]
        sums = pltpu.roll(jnp.where(first_lanes, b, a), HEAD_DIM, axis=1)
        o_ref[rows, :] = (jnp.where(first_lanes, a, b) / sums).astype(o_ref.dtype)
        return carry

    lax.fori_loop(0, nq, finish, 0, unroll=4)


def _prompt_attn(q, ktb, vtb, ct, *, bq, bk):
    b, s, _ = q.shape
    nq = s // bq
    per = bk // bq
    assert s % bk == 0 and bk % bq == 0 and per & (per - 1) == 0 and bq & (bq - 1) == 0
    n_pairs = N_HEADS // HEADS_PER_STEP
    ct = ct.reshape(b, n_pairs, HEADS_PER_STEP, s)
    below = [(i, j) for i in range(nq) for j in range(i // per)] or [(0, 0)]
    pi = jnp.asarray(np.array([p[0] for p in below], np.int32))
    pj = jnp.asarray(np.array([p[1] for p in below], np.int32))
    qspec = pl.BlockSpec((None, s, LANES), lambda bi, hp, *_: (bi, 0, hp))
    kvspec = pl.BlockSpec((None, LANES, s), lambda bi, hp, *_: (bi, hp, 0))
    return pl.pallas_call(
        functools.partial(_prompt_attn_kernel, bq=bq, bk=bk),
        grid_spec=pltpu.PrefetchScalarGridSpec(
            num_scalar_prefetch=2,
            grid=(b, n_pairs),
            in_specs=[qspec, kvspec, kvspec,
                      pl.BlockSpec((None, None, HEADS_PER_STEP, s), lambda bi, hp, *_: (bi, hp, 0, 0))],
            out_specs=qspec,
            scratch_shapes=[pltpu.VMEM((nq, HEADS_PER_STEP * bq, MXU_DIM), BF16),
                            pltpu.VMEM((MXU_DIM, s), BF16),
                            pltpu.VMEM((HEADS_PER_STEP, LANES, s), BF16),
                            pltpu.VMEM((HEADS_PER_STEP, s, LANES), F32),
                            pltpu.VMEM((HEADS_PER_STEP, s, LANES), F32),
                            pltpu.VMEM((2, HEADS_PER_STEP * bq, bk), F32),
                            pltpu.VMEM((2, HEADS_PER_STEP, bq, bk), BF16),
                            pltpu.VMEM((2, HEADS_PER_STEP, bq, LANES), F32)],
        ),
        out_shape=jax.ShapeDtypeStruct((b, s, ATT_WIDTH), BF16),
        compiler_params=_params(("arbitrary", "arbitrary")),
        name="prompt_attn",
    )(pi, pj, q, ktb, vtb, ct)


def _decay_kernel(pt_ref, lf_hbm, wcat_ref, psuf_ref, o_ref, x_ref, sem, *, n_pool):
    d = pl.program_id(0)
    b = pl.program_id(1)
    nb = pl.num_programs(1)
    step = d * nb + b
    slot = step & 1
    n_pages = x_ref.shape[2]

    def page_copy(p, page, slot_):
        return pltpu.make_async_copy(lf_hbm.at[page], x_ref.at[slot_, :, pl.ds(p, 1), :], sem.at[slot_])

    def fetch(d_, b_, slot_):
        def start(p, carry):
            page_copy(p, pt_ref[b_, p] + d_ * n_pool, slot_).start()
            return carry
        lax.fori_loop(0, n_pages, start, 0, unroll=8)

    @pl.when(step == 0)
    def _():
        fetch(d, b, slot)

    @pl.when(step + 1 < pl.num_programs(0) * nb)
    def _():
        wrap = b + 1 == nb
        fetch(jnp.where(wrap, d + 1, d), jnp.where(wrap, 0, b + 1), 1 - slot)

    def wait(p, carry):
        page_copy(p, 0, slot).wait()
        return carry

    lax.fori_loop(0, n_pages, wait, 0, unroll=8)

    page = x_ref.shape[3]
    xs = _split3(x_ref[slot].reshape(N_HEADS * n_pages, page))
    wc = sum(jnp.dot(t, wcat_ref[...], preferred_element_type=F32) for t in xs)
    head = lambda h: slice(h * n_pages, (h + 1) * n_pages)
    totals = jnp.concatenate([wc[head(h), page:] for h in range(N_HEADS)], axis=1)
    later = sum(jnp.dot(psuf_ref[...], t, preferred_element_type=F32) for t in _split3(totals))
    for h in range(N_HEADS):
        o_ref[h] = wc[head(h), :page] + later[:, h * page:(h + 1) * page]


def _decay_bias(page_table, logf_pages):
    depth, n_pool, n_heads, page = logf_pages.shape
    db, n_pages = page_table.shape
    t = jnp.arange(page)
    tsuf = (t[:, None] > t[None, :]).astype(BF16)
    wcat = jnp.concatenate([tsuf, jnp.ones((page, page), BF16)], axis=1)
    pidx = jnp.arange(n_pages)
    psuf = (pidx[None, :] > pidx[:, None]).astype(BF16)
    const = lambda shape: pl.BlockSpec(shape, lambda d, b, pt: (0,) * len(shape), pipeline_mode=pl.Buffered(1))
    return pl.pallas_call(
        functools.partial(_decay_kernel, n_pool=n_pool),
        grid_spec=pltpu.PrefetchScalarGridSpec(
            num_scalar_prefetch=1,
            grid=(depth, db),
            in_specs=[pl.BlockSpec(memory_space=pl.ANY), const(wcat.shape), const(psuf.shape)],
            out_specs=pl.BlockSpec((None, None, n_heads, n_pages, page), lambda d, b, pt: (d, b, 0, 0, 0)),
            scratch_shapes=[pltpu.VMEM((2, n_heads, n_pages, page), F32), pltpu.SemaphoreType.DMA((2,))],
        ),
        out_shape=jax.ShapeDtypeStruct((depth, db, n_heads, n_pages, page), F32),
        compiler_params=_params(("arbitrary", "arbitrary")),
        name="decay_bias",
    )(page_table, logf_pages.reshape(depth * n_pool, n_heads, 1, page), wcat, psuf)


def _expand_heads(row_of_head, reps):
    return jnp.concatenate([jnp.broadcast_to(row_of_head(h), (reps, row_of_head(h).shape[1]))
                            for h in range(N_HEADS)], axis=0)


def _sample_attn_kernel(pt_ref, q_ref, kt_hbm, vt_hbm, bias_ref, kn_ref, vn_ref, lfn_ref, o_ref,
                        kbuf_ref, vbuf_ref, sem, qbd_ref, acc_ref, m_ref, l_ref, s_ref, kpad_ref, vpad_ref,
                        *, pps, t_new, base, lookahead):
    b = pl.program_id(0)
    j = pl.program_id(1)
    nb = pl.num_programs(0)
    n_steps = pl.num_programs(1) - 1
    n_slots = kbuf_ref.shape[0]
    chunk = b * n_steps + jnp.minimum(j, n_steps - 1)
    slot_of = lambda c: lax.rem(c, n_slots)

    def chunk_copies(bb, jj, slot, known_pages):
        copies = []
        for r in range(pps):
            pg = pt_ref[bb, jj * pps + r] + base if known_pages else 0
            copies.append(pltpu.make_async_copy(kt_hbm.at[pg], kbuf_ref.at[slot, r], sem.at[slot]))
            copies.append(pltpu.make_async_copy(vt_hbm.at[pg], vbuf_ref.at[slot, r], sem.at[slot]))
        return copies

    @pl.when((b == 0) & (j == 0))
    def _():
        for x in range(lookahead):
            for c in chunk_copies(0, x, x, True):
                c.start()

    @pl.when(j < n_steps)
    def _():
        ahead = j + lookahead
        wrap = ahead >= n_steps
        b_ahead = jnp.where(wrap, b + 1, b)
        j_ahead = jnp.where(wrap, ahead - n_steps, ahead)

        @pl.when(b_ahead < nb)
        def _():
            for c in chunk_copies(b_ahead, j_ahead, slot_of(chunk + lookahead), True):
                c.start()

        for c in chunk_copies(b, j, slot_of(chunk), False):
            c.wait()

    slot = slot_of(chunk)
    vslot = jnp.where((j == 0) | (j == n_steps), slot, slot_of(chunk + n_slots - 1))

    @pl.when(j == 0)
    def _():
        qt = jnp.concatenate([q_ref[...].astype(F32)] * N_HEADS, axis=0)
        r = lax.broadcasted_iota(jnp.int32, qt.shape, 0) // t_new
        c = lax.broadcasted_iota(jnp.int32, qt.shape, 1) // HEAD_DIM
        qbd_ref[...] = jnp.where(r == c, qt, 0.0).astype(BF16)
        acc_ref[...] = jnp.zeros_like(acc_ref)
        l_ref[...] = jnp.zeros_like(l_ref)
        m_ref[...] = jnp.full_like(m_ref, NEG_INF)
        s_ref[...] = jnp.full_like(s_ref, NEG_INF)

    def update(scores, weighted_values):
        m_prev = m_ref[...]
        m_cur = scores[0].max(axis=1, keepdims=True)
        for s in scores[1:]:
            m_cur = jnp.maximum(m_cur, s.max(axis=1, keepdims=True))
        m_new = jnp.maximum(m_prev, m_cur)
        alpha = jnp.exp(m_prev - m_new)
        l_new = alpha * l_ref[...]
        acc = alpha[:, :1] * acc_ref[...]
        for s, pv in zip(scores, weighted_values):
            p = jnp.exp(s - m_new[:, :1])
            l_new = l_new + p.sum(axis=1, keepdims=True)
            acc = acc + pv(p.astype(BF16))
        m_ref[...] = m_new
        l_ref[...] = l_new
        acc_ref[...] = acc

    waiting = [s_ref[r] for r in range(pps)]
    qbd = qbd_ref[...]
    for r in range(pps):
        s = jnp.dot(qbd, kbuf_ref[slot, r].astype(BF16), preferred_element_type=F32)
        s_ref[r] = s + _expand_heads(lambda h: bias_ref[h, r:r + 1, :], t_new)
    update(waiting,
           [lambda p, r=r: lax.dot_general(p, vbuf_ref[vslot, r].astype(BF16), NT_DIMS, preferred_element_type=F32)
            for r in range(pps)])

    @pl.when(j == pl.num_programs(1) - 1)
    def _():
        kpad_ref[...] = jnp.zeros_like(kpad_ref)
        vpad_ref[...] = jnp.zeros_like(vpad_ref)
        kpad_ref[:t_new, :] = kn_ref[...]
        vpad_ref[:t_new, :] = vn_ref[...]
        cn = _lane_cumsum(lfn_ref[...])
        s = lax.dot_general(qbd, kpad_ref[...].astype(BF16), NT_DIMS, preferred_element_type=F32)
        s = s - _expand_heads(lambda h: cn[h:h + 1, :], t_new)
        qi = lax.broadcasted_iota(jnp.int32, s.shape, 0) % t_new
        ki = lax.broadcasted_iota(jnp.int32, s.shape, 1)
        s = jnp.where(ki <= qi, s, NEG_INF)
        update([s], [lambda p: jnp.dot(p, vpad_ref[...].astype(BF16), preferred_element_type=F32)])
        out = acc_ref[...] / l_ref[:, :1]
        res = jnp.zeros((t_new, ATT_WIDTH), F32)
        lane_head = lax.broadcasted_iota(jnp.int32, res.shape, 1) // HEAD_DIM
        for h in range(N_HEADS):
            res = jnp.where(lane_head == h, out[h * t_new:(h + 1) * t_new, :], res)
        o_ref[...] = res.astype(o_ref.dtype)


def _sample_attn(page_table, q, k_new, v_new, lf_new, kt_pages, vt_pages, bias, layer, *, pps):
    db, t_new, _ = q.shape
    n_pages = page_table.shape[1]
    page = kt_pages.shape[2]
    n_pool = kt_pages.shape[0] // bias.shape[0]
    base = layer * n_pool
    rows = N_HEADS * t_new
    n_steps = n_pages // pps
    lookahead = min(2, n_steps)
    n_slots = lookahead + 2
    key_step = lambda j: jnp.minimum(j, n_steps - 1)
    tok = pl.BlockSpec((None, t_new, ATT_WIDTH), lambda b, j, pt: (b, 0, 0))
    hbm = pl.BlockSpec(memory_space=pl.ANY)
    in_specs = [tok, hbm, hbm,
                pl.BlockSpec((None, None, N_HEADS, pps, page), lambda b, j, pt: (layer, b, 0, key_step(j), 0)),
                tok, tok, pl.BlockSpec((None, N_HEADS, LANES), lambda b, j, pt: (b, 0, 0))]
    return pl.pallas_call(
        functools.partial(_sample_attn_kernel, pps=pps, t_new=t_new, base=base, lookahead=lookahead),
        grid_spec=pltpu.PrefetchScalarGridSpec(
            num_scalar_prefetch=1,
            grid=(db, n_steps + 1),
            in_specs=in_specs,
            out_specs=tok,
            scratch_shapes=[pltpu.VMEM((n_slots, pps, ATT_WIDTH, page), F32),
                            pltpu.VMEM((n_slots, pps, ATT_WIDTH, page), F32),
                            pltpu.SemaphoreType.DMA((n_slots,)),
                            pltpu.VMEM((rows, ATT_WIDTH), BF16), pltpu.VMEM((rows, ATT_WIDTH), F32),
                            pltpu.VMEM((rows, LANES), F32), pltpu.VMEM((rows, LANES), F32),
                            pltpu.VMEM((pps, rows, page), F32),
                            pltpu.VMEM((page, ATT_WIDTH), F32), pltpu.VMEM((page, ATT_WIDTH), F32)],
        ),
        out_shape=jax.ShapeDtypeStruct((db, t_new, ATT_WIDTH), BF16),
        compiler_params=_params(("arbitrary", "arbitrary")),
        name="sample_attn",
    )(page_table, q, kt_pages, vt_pages, bias, k_new, v_new, lf_new)


def _pool_kernel(u_ref, hist_ref, o_ref, z_ref, *, pos0):
    z_ref[:HIST_ROWS, :] = hist_ref[...]
    z_ref[HIST_ROWS:, :] = u_ref[...]
    _pooled_rows(z_ref, o_ref, pos0)


def _pooled_rows(z_ref, o_ref, pos0):
    tm = o_ref.shape[0]
    row = lax.broadcasted_iota(jnp.int32, (tm, 1), 0)
    pos_count = pos0 + row + 1
    for g, w in enumerate(POOL_WINDOWS):
        cols = slice(g * POOL_GROUP, (g + 1) * POOL_GROUP)
        total = z_ref[:, cols]
        shift = 1
        while shift < w:
            total = total + pltpu.roll(total, shift, axis=0)
            shift *= 2
        cnt = jnp.minimum(pos_count, w).astype(F32)
        o_ref[:, cols] = (total[HIST_ROWS:, :] / cnt - z_ref[HIST_ROWS:, cols]).astype(o_ref.dtype)


def _pool_means(u, hist, *, pos0):
    nb, t, _ = u.shape
    row = pl.BlockSpec((None, t, POOL_WIDTH), lambda b: (b, 0, 0))
    return pl.pallas_call(
        functools.partial(_pool_kernel, pos0=pos0),
        grid=(nb,),
        in_specs=[row, pl.BlockSpec((None, HIST_ROWS, POOL_WIDTH), lambda b: (b, 0, 0))],
        out_specs=row,
        out_shape=jax.ShapeDtypeStruct(u.shape, BF16),
        scratch_shapes=[pltpu.VMEM((HIST_ROWS + t, POOL_WIDTH), F32)],
        compiler_params=_params(("arbitrary",)),
        name="pool_means",
    )(u, hist)


def _post_kernel(h_ref, att_ref, pooled_ref, p_ref, wpool_ref, pscale_ref, gfox_ref, gpool_ref, wout_ref,
                 gpostmix_ref, gpremlp_ref, wup_ref, wdown_ref, gpostmlp_ref, wple_ref, gple_ref, wgate_ref,
                 o_ref, acc_ref, *, ff_chunk):
    dot = functools.partial(jnp.dot, preferred_element_type=F32)
    an = _rms(att_ref[...].astype(F32), gfox_ref[...]).astype(BF16)
    pooled = pooled_ref[...]
    ys = [dot(pooled[:, g * POOL_GROUP:(g + 1) * POOL_GROUP], wpool_ref[g]) for g in range(len(POOL_WINDOWS))]
    y = jnp.concatenate(ys, axis=-1) * pscale_ref[...]
    pn = _rms(y, gpool_ref[...]).astype(BF16)
    mix = dot(an, wout_ref[:ATT_WIDTH, :]) + dot(pn, wout_ref[ATT_WIDTH:, :])
    h1 = h_ref[...] + _rms(mix, gpostmix_ref[...])
    m = _rms(h1, gpremlp_ref[...]).astype(BF16)
    d_ff = wup_ref.shape[1]
    for c in range(d_ff // ff_chunk):
        cols = slice(c * ff_chunk, (c + 1) * ff_chunk)
        hidden = jnp.square(jnp.maximum(dot(m, wup_ref[:, cols]), 0.0)).astype(BF16)
        part = dot(hidden, wdown_ref[cols, :])
        if c == 0:
            acc_ref[...] = part
        else:
            acc_ref[...] += part
    h2 = h1 + _rms(acc_ref[...], gpostmlp_ref[...])
    e = _rms(dot(p_ref[...].astype(BF16), wple_ref[...]), gple_ref[...])
    gate = jax.nn.sigmoid(dot(h2.astype(BF16), wgate_ref[...]))
    o_ref[...] = h2 + gate * e


def _post(h, att, pooled, p, w, *, tm, ff_chunk):
    n, d = h.shape
    row = lambda width: pl.BlockSpec((tm, width), lambda i: (i, 0))
    weights = (w["w_pool"], w["pool_scale"], w["g_fox"], w["g_pool"], w["w_out"], w["g_post_mix"], w["g_pre_mlp"],
               w["w_up"], w["w_down"], w["g_post_mlp"], w["w_ple"], w["g_ple"], w["w_ple_gate"])
    return pl.pallas_call(
        functools.partial(_post_kernel, ff_chunk=ff_chunk),
        grid=(n // tm,),
        in_specs=[row(d), row(ATT_WIDTH), row(POOL_WIDTH), row(p.shape[1])] + [_const_spec(x.shape) for x in weights],
        out_specs=row(d),
        out_shape=jax.ShapeDtypeStruct((n, d), F32),
        scratch_shapes=[pltpu.VMEM((tm, d), F32)],
        compiler_params=_params(("arbitrary",)),
        name="post",
    )(h, att, pooled, p, *weights)


def _tile(n, target):
    t = min(n, target)
    assert n % t == 0, (n, t)
    return t


def kernel(x_prompt, x_sample, cache_k, cache_v, cache_logf, state_pool, page_table, p_prompt, p_sample, w_in, b_f, w_pool, pool_scale, g_fox, g_pool, w_out, g_pre_mix, g_post_mix, g_pre_mlp, g_post_mlp, w_up, w_down, w_ple, g_ple, w_ple_gate):
    b, s, d = x_prompt.shape
    db, t_new, _ = x_sample.shape
    depth, n_pool, page, n_heads, head_dim = cache_k.shape
    n_pages = page_table.shape[1]
    assert (n_heads, head_dim) == (N_HEADS, HEAD_DIM) and w_in.shape[2] == 3 * ATT_WIDTH + POOL_WIDTH + N_HEADS
    assert page == LANES and t_new == 8 and state_pool.shape[2:] == (POOL_BUF, POOL_WIDTH)
    w = ATT_WIDTH
    n_main = 3 * w + POOL_WIDTH
    past = n_pages * page

    w_bf = w_in.astype(BF16)
    w_rows = w_bf[:, :, :n_main]
    w_qu = jnp.concatenate([w_bf[:, :, :w], w_bf[:, :, 3 * w:n_main]], axis=2)
    w_t = jnp.swapaxes(w_bf, 1, 2)
    w_kvt = w_t[:, w:3 * w]
    w_ft = w_t[:, n_main:]
    vec = lambda x: x[:, None, :]
    layers = []
    for i in range(depth):
        layers.append(dict(
            w_pool=w_pool[i].astype(BF16), pool_scale=vec(pool_scale)[i], g_fox=vec(g_fox)[i], g_pool=vec(g_pool)[i],
            w_out=w_out[i].astype(BF16), g_post_mix=vec(g_post_mix)[i], g_pre_mlp=vec(g_pre_mlp)[i],
            w_up=w_up[i].astype(BF16), w_down=w_down[i].astype(BF16), g_post_mlp=vec(g_post_mlp)[i],
            w_ple=w_ple[i].astype(BF16), g_ple=vec(g_ple)[i], w_ple_gate=w_ple_gate[i].astype(BF16)))

    kt_pages = jnp.transpose(cache_k, (0, 1, 3, 4, 2)).reshape(depth * n_pool, w, page)
    vt_pages = jnp.transpose(cache_v, (0, 1, 3, 4, 2)).reshape(depth * n_pool, w, page)
    bias = _decay_bias(page_table, jnp.swapaxes(cache_logf, 2, 3))

    tm_in = _tile(s, 512)
    bq = _tile(s, MXU_DIM)
    bk = _tile(s, 2 * MXU_DIM)
    tm_post = _tile(b * s, 512)
    pps = _tile(n_pages, 8)
    n_s = db * t_new
    tok3 = lambda x: x.reshape(db, t_new, x.shape[-1])

    hp = x_prompt
    hs = x_sample.reshape(1, n_s, d)
    kv_bufs = None
    pool_p, k_s, v_s, lf_s_all, pool_s = [], [], [], [], []
    for i in range(depth):
        g_pre = vec(g_pre_mix)[i]
        bf_col = b_f[i][:, None]
        q, pooled, u_tail, kt_all, vt_all, ktb, vtb, lft_all, ct = _inproj_cols(
            hp, g_pre, w_qu[i], w_kvt[i], w_ft[i], bf_col, i, depth, kv_bufs, tm=tm_in)
        kv_bufs = (kt_all, vt_all, lft_all)
        att = _prompt_attn(q, ktb, vtb, ct, bq=bq, bk=bk)
        hp = _post(hp.reshape(b * s, d), att.reshape(b * s, w), pooled.reshape(b * s, POOL_WIDTH),
                   p_prompt[i].reshape(b * s, -1), layers[i], tm=tm_post, ff_chunk=512).reshape(b, s, d)
        pool_p.append(u_tail[:, HIST_ROWS - POOL_BUF:, :])
        qs, ks, vs, us, lfts = _inproj_rows(hs, g_pre, w_rows[i], w_ft[i], bf_col)
        lf_s = jnp.swapaxes(lfts[0], 0, 1).reshape(db, t_new, N_HEADS)
        lf_new = jnp.pad(jnp.swapaxes(lf_s, 1, 2), ((0, 0), (0, 0), (0, LANES - t_new)))
        att_s = _sample_attn(page_table, tok3(qs[0]), tok3(ks[0]), tok3(vs[0]), lf_new, kt_pages, vt_pages, bias, i,
                             pps=pps)
        hist = jnp.pad(state_pool[i], ((0, 0), (1, 0), (0, 0)))
        pooled_s = _pool_means(tok3(us[0]), hist, pos0=past)
        hs = _post(hs.reshape(n_s, d), att_s.reshape(n_s, w), pooled_s.reshape(n_s, POOL_WIDTH),
                   p_sample[i].reshape(n_s, -1), layers[i], tm=n_s, ff_chunk=512).reshape(1, n_s, d)
        k_s.append(ks.reshape(db, t_new, N_HEADS, HEAD_DIM))
        v_s.append(vs.reshape(db, t_new, N_HEADS, HEAD_DIM))
        lf_s_all.append(lf_s)
        pool_s.append(jnp.concatenate([state_pool[i], tok3(us[0])], axis=1)[:, -POOL_BUF:])
    kt_all, vt_all, lft_all = kv_bufs
    heads_last = lambda xt: jnp.transpose(xt.reshape(depth, b, N_HEADS, HEAD_DIM, s), (0, 1, 4, 2, 3))
    return (hp, hs.reshape(db, t_new, d), heads_last(kt_all), heads_last(vt_all), jnp.swapaxes(lft_all, 2, 3),
            jnp.stack(pool_p), jnp.stack(k_s), jnp.stack(v_s), jnp.stack(lf_s_all), jnp.stack(pool_s))
```

```python
import functools
import math

import jax
import jax.numpy as jnp
import numpy as np
from jax import lax
from jax.experimental import pallas as pl
from jax.experimental.pallas import tpu as pltpu

F32 = jnp.float32
BF16 = jnp.bfloat16

EPS = 1e-6
NEG_INF = -1e30
HEAD_DIM = 64
N_HEADS = 8
ATT_WIDTH = N_HEADS * HEAD_DIM
POOL_WINDOWS = (2, 4, 8, 16)
POOL_GROUP = 128
POOL_WIDTH = POOL_GROUP * len(POOL_WINDOWS)
POOL_BUF = max(POOL_WINDOWS) - 1
HIST_ROWS = POOL_BUF + 1
ATT_SCALE = 1.0 / math.sqrt(HEAD_DIM)

LANES = 128
MXU_DIM = 256
HEADS_PER_STEP = LANES // HEAD_DIM
MAX_TICKS_PER_TRIP = 18
VMEM_LIMIT = 56 * 1024 * 1024

NT_DIMS = (((1,), (1,)), ((), ()))


def _params(semantics):
    return pltpu.CompilerParams(dimension_semantics=semantics, vmem_limit_bytes=VMEM_LIMIT)


def _rms(x, g):
    return x * lax.rsqrt(jnp.mean(x * x, axis=-1, keepdims=True) + EPS) * g


def _const_spec(shape):
    zeros = (0,) * len(shape)
    return pl.BlockSpec(shape, lambda *_: zeros, pipeline_mode=pl.Buffered(1))


def _lane_cumsum(x):
    n = x.shape[1]
    lane = lax.broadcasted_iota(jnp.int32, x.shape, 1)
    shift = 1
    while shift < n:
        x = x + jnp.where(lane >= shift, pltpu.roll(x, shift, axis=1), 0.0)
        shift *= 2
    return x


def _split3(x):
    x1 = x.astype(BF16)
    r1 = x - x1.astype(F32)
    x2 = r1.astype(BF16)
    x3 = (r1 - x2.astype(F32)).astype(BF16)
    return x1, x2, x3


def _inproj_kernel(h_ref, g_ref, w_ref, wt_ref, wft_ref, bf_ref, *refs, transposed_kv, n_alias=0, layer=None):
    refs = refs[n_alias:]

    def put(ref, val):
        if layer is None:
            ref[...] = val
        else:
            for l in range(ref.shape[0]):
                ref[l] = val if l == layer else jnp.zeros_like(val)

    if transposed_kv:
        @pl.when(pl.program_id(1) == 0)
        def _():
            refs[-2][...] = jnp.zeros_like(refs[-2])
            refs[-1][:HIST_ROWS, :] = jnp.zeros((HIST_ROWS, POOL_WIDTH), F32)

    a = _rms(h_ref[...], g_ref[...]).astype(BF16)
    w = ATT_WIDTH
    f = lax.dot_general(wft_ref[...], a, NT_DIMS, preferred_element_type=F32) + bf_ref[...]
    lf = jnp.minimum(f, 0.0) - jnp.log1p(jnp.exp(-jnp.abs(f)))
    if transposed_kv:
        q_ref, pooled_ref, utail_ref, kt_ref, vt_ref, ktb_ref, vtb_ref, lft_ref, ct_ref, carry_ref, z_ref = refs
        tm = q_ref.shape[0]
        put(lft_ref, lf)
        c = _lane_cumsum(lf) + carry_ref[:, :1]
        ct_ref[...] = c
        carry_ref[...] = jnp.broadcast_to(c[:, -1:], carry_ref.shape)
        z = jnp.dot(a, w_ref[...], preferred_element_type=F32)
        q_ref[...] = (z[:, :w] * ATT_SCALE).astype(BF16)
        z_ref[HIST_ROWS:, :] = z[:, w:]
        _pooled_rows(z_ref, pooled_ref, pl.program_id(1) * tm)
        tail = z_ref[tm:, :]
        utail_ref[...] = tail
        z_ref[:HIST_ROWS, :] = tail
        zt = lax.dot_general(wt_ref[...], a, NT_DIMS, preferred_element_type=F32)
        put(kt_ref, zt[:w])
        put(vt_ref, zt[w:])
        ktb_ref[...] = zt[:w].astype(BF16)
        vtb_ref[...] = zt[w:].astype(BF16)
    else:
        z = jnp.dot(a, w_ref[...], preferred_element_type=F32)
        q_ref, k_ref, v_ref, u_ref, lft_ref = refs
        q_ref[...] = (z[:, :w] * ATT_SCALE).astype(BF16)
        k_ref[...] = z[:, w:2 * w]
        v_ref[...] = z[:, 2 * w:3 * w]
        u_ref[...] = z[:, 3 * w:]
        lft_ref[...] = lf


def _inproj_rows(h, g, w_rows, w_ft, b_f):
    nb, s, d = h.shape
    row = lambda width: pl.BlockSpec((None, s, width), lambda b, i: (b, 0, 0))
    sds = jax.ShapeDtypeStruct
    return pl.pallas_call(
        functools.partial(_inproj_kernel, transposed_kv=False, n_alias=0),
        grid=(nb, 1),
        in_specs=[row(d), _const_spec((1, d)), _const_spec(w_rows.shape), _const_spec(w_ft.shape),
                  _const_spec((N_HEADS, d)), _const_spec((N_HEADS, 1))],
        out_specs=[row(ATT_WIDTH)] * 3 + [row(POOL_WIDTH), pl.BlockSpec((None, N_HEADS, s), lambda b, i: (b, 0, 0))],
        out_shape=[sds((nb, s, ATT_WIDTH), BF16), sds((nb, s, ATT_WIDTH), F32), sds((nb, s, ATT_WIDTH), F32),
                   sds((nb, s, POOL_WIDTH), F32), sds((nb, N_HEADS, s), F32)],
        compiler_params=_params(("arbitrary", "arbitrary")),
        name="inproj_rows",
    )(h, g, w_rows, w_ft, w_ft, b_f)


def _inproj_cols(h, g, w_qu, w_kvt, w_ft, b_f, layer, depth, kv_prev, *, tm):
    nb, s, d = h.shape
    row = lambda width: pl.BlockSpec((None, tm, width), lambda b, i: (b, i, 0))
    col = lambda height: pl.BlockSpec((None, height, tm), lambda b, i: (b, 0, i))
    if kv_prev is None:
        layer_col = lambda height: pl.BlockSpec((depth, None, height, tm), lambda b, i: (0, b, 0, i))
    else:
        layer_col = lambda height: pl.BlockSpec((None, None, height, tm), lambda b, i: (layer, b, 0, i))
    sds = jax.ShapeDtypeStruct
    out_shape = [sds((nb, s, ATT_WIDTH), BF16), sds((nb, s, POOL_WIDTH), BF16), sds((nb, HIST_ROWS, POOL_WIDTH), F32),
                 sds((depth, nb, ATT_WIDTH, s), F32), sds((depth, nb, ATT_WIDTH, s), F32),
                 sds((nb, ATT_WIDTH, s), BF16), sds((nb, ATT_WIDTH, s), BF16),
                 sds((depth, nb, N_HEADS, s), F32), sds((nb, N_HEADS, s), F32)]
    out_specs = [row(ATT_WIDTH), row(POOL_WIDTH), pl.BlockSpec((None, HIST_ROWS, POOL_WIDTH), lambda b, i: (b, 0, 0)),
                 layer_col(ATT_WIDTH), layer_col(ATT_WIDTH),
                 col(ATT_WIDTH), col(ATT_WIDTH), layer_col(N_HEADS), col(N_HEADS)]
    in_specs = [row(d), _const_spec((1, d)), _const_spec(w_qu.shape), _const_spec(w_kvt.shape),
                _const_spec((N_HEADS, d)), _const_spec((N_HEADS, 1))]
    args = [h, g, w_qu, w_kvt, w_ft, b_f]
    aliases = {}
    if kv_prev is not None:
        for out_idx, buf in zip((3, 4, 7), kv_prev):
            aliases[len(args)] = out_idx
            in_specs.append(pl.BlockSpec(memory_space=pl.ANY))
            args.append(buf)
    return pl.pallas_call(
        functools.partial(_inproj_kernel, transposed_kv=True, n_alias=len(aliases),
                          layer=layer if kv_prev is None else None),
        grid=(nb, s // tm),
        in_specs=in_specs,
        out_specs=out_specs,
        out_shape=out_shape,
        scratch_shapes=[pltpu.VMEM((N_HEADS, LANES), F32), pltpu.VMEM((HIST_ROWS + tm, POOL_WIDTH), F32)],
        input_output_aliases=aliases,
        compiler_params=_params(("arbitrary", "arbitrary")),
        name="inproj_cols",
    )(*args)


def _prompt_attn_kernel(pi_ref, pj_ref, q_ref, kt_ref, vt_ref, c_ref, o_ref,
                        qs_ref, kta_ref, vta_ref, acc_ref, m_ref, s_ref, p_ref, alpha_ref, *, bq, bk):
    s_len = q_ref.shape[0]
    nq = s_len // bq
    per = bk // bq
    first_lanes = lax.broadcasted_iota(jnp.int32, (1, LANES), 1) < HEAD_DIM
    first_rows = lax.broadcasted_iota(jnp.int32, (LANES, 1), 0) < HEAD_DIM

    q2 = q_ref[...].reshape(nq, bq, LANES)
    zero = jnp.zeros_like(q2)
    lane = lax.broadcasted_iota(jnp.int32, (bq, LANES), 1)
    ones_a = jnp.where(lane < 3, 1.0, 0.0).astype(BF16)
    ones_b = jnp.where((lane >= 3) & (lane < 6), 1.0, 0.0).astype(BF16)
    first_q = (lane < HEAD_DIM)[None]
    qs_ref[:, :bq, :LANES] = jnp.where(first_q, q2, zero)
    qs_ref[:, bq:, :LANES] = jnp.where(first_q, zero, q2)

    @pl.when((pl.program_id(0) == 0) & (pl.program_id(1) == 0))
    def _():
        qs_ref[:, :bq, LANES:] = jnp.broadcast_to(ones_a[None], (nq, bq, LANES))
        qs_ref[:, bq:, LANES:] = jnp.broadcast_to(ones_b[None], (nq, bq, LANES))
        kta_ref[LANES + 16:, :] = jnp.zeros((LANES - 16, s_len), BF16)

    kta_ref[:LANES, :] = kt_ref[...]
    neg_c = -c_ref[...]
    parts = _split3(neg_c[0:1, :]) + _split3(neg_c[1:2, :])
    r16 = lax.broadcasted_iota(jnp.int32, (16, s_len), 0)
    extra = jnp.zeros((16, s_len), F32)
    for idx, part in enumerate(parts):
        extra = jnp.where(r16 == idx, part.astype(F32), extra)
    kta_ref[LANES:LANES + 16, :] = extra.astype(BF16)
    vt = vt_ref[...]
    one = jnp.ones_like(vt)
    vta_ref[0] = jnp.where(first_rows, vt, one)
    vta_ref[1] = jnp.where(first_rows, one, vt)

    def scores(i, j, slot, masked):
        cols = pl.ds(pl.multiple_of(j * bk, bk), bk)
        s2 = jnp.dot(qs_ref[i], kta_ref[:, cols], preferred_element_type=F32)
        if masked:
            r = (lax.broadcasted_iota(jnp.int32, s2.shape, 0) & (bq - 1)) + (i * bq - j * bk)
            c = lax.broadcasted_iota(jnp.int32, s2.shape, 1)
            s2 = jnp.where(c <= r, s2, NEG_INF)
        s_ref[slot] = s2

    def softmax(i, slot, first):
        rows = pl.ds(pl.multiple_of(i * bq, bq), bq)
        for hh in range(HEADS_PER_STEP):
            head = slice(hh * bq, (hh + 1) * bq)
            m_new = jnp.broadcast_to(jnp.max(s_ref[slot, head, :], axis=1, keepdims=True), (bq, LANES))
            if not first:
                m_prev = m_ref[hh, rows, :]
                m_new = jnp.maximum(m_prev, m_new)
                alpha_ref[slot, hh] = jnp.exp(m_prev - m_new)
            p_ref[slot, hh] = jnp.concatenate(
                [jnp.exp(s_ref[slot, head, k * LANES:(k + 1) * LANES] - m_new) for k in range(bk // LANES)],
                axis=1).astype(BF16)
            m_ref[hh, rows, :] = m_new

    def accumulate(i, j, slot, first):
        rows = pl.ds(pl.multiple_of(i * bq, bq), bq)
        cols = pl.ds(pl.multiple_of(j * bk, bk), bk)
        for hh in range(HEADS_PER_STEP):
            pv = lax.dot_general(p_ref[slot, hh], vta_ref[hh, :, cols], NT_DIMS, preferred_element_type=F32)
            acc_ref[hh, rows, :] = pv if first else acc_ref[hh, rows, :] * alpha_ref[slot, hh] + pv

    def sweep(n, pair, masked):
        first = masked
        if n == 0:
            return
        scores(*pair(0), 0, masked)
        if n == 1:
            softmax(pair(0)[0], 0, first)
            accumulate(*pair(0), 0, first)
            return
        softmax(pair(0)[0], 0, first)
        scores(*pair(1), 1, masked)

        def tick(t, slot):
            accumulate(*pair(t - 2), slot, first)
            softmax(pair(t - 1)[0], 1 - slot, first)
            scores(*pair(t), slot, masked)

        per_trip = max([x for x in range(2, MAX_TICKS_PER_TRIP + 1, 2) if (n - 2) % x == 0], default=2)

        def ticks(k, carry):
            for x in range(per_trip):
                tick(per_trip * k + 2 + x, x % 2)
            return carry

        trips = (n - 2) // per_trip
        lax.fori_loop(0, trips, ticks, 0)
        for t in range(2 + trips * per_trip, n):
            tick(t, t % 2)
        accumulate(*pair(n - 2), n % 2, first)
        softmax(pair(n - 1)[0], (n - 1) % 2, first)
        accumulate(*pair(n - 1), (n - 1) % 2, first)

    def last_key_block(i):
        return i // per if isinstance(i, int) else lax.shift_right_logical(i, per.bit_length() - 1)

    sweep(nq, lambda t: (t, last_key_block(t)), True)
    sweep(sum(i // per for i in range(nq)), lambda t: (pi_ref[t], pj_ref[t]), False)

    def finish(i, carry):
        rows = pl.ds(pl.multiple_of(i * bq, bq), bq)
        a = acc_ref[0, rows, :]
        b = acc_ref[1, rows, :]
        sums = pltpu.roll(jnp.where(first_lanes, b, a), HEAD_DIM, axis=1)
        o_ref[rows, :] = (jnp.where(first_lanes, a, b) / sums).astype(o_ref.dtype)
        return carry

    lax.fori_loop(0, nq, finish, 0, unroll=4)


def _prompt_attn(q, ktb, vtb, ct, *, bq, bk):
    b, s, _ = q.shape
    nq = s // bq
    per = bk // bq
    assert s % bk == 0 and bk % bq == 0 and per & (per - 1) == 0 and bq & (bq - 1) == 0
    n_pairs = N_HEADS // HEADS_PER_STEP
    ct = ct.reshape(b, n_pairs, HEADS_PER_STEP, s)
    below = [(i, j) for i in range(nq) for j in range(i // per)] or [(0, 0)]
    pi = jnp.asarray(np.array([p[0] for p in below], np.int32))
    pj = jnp.asarray(np.array([p[1] for p in below], np.int32))
    qspec = pl.BlockSpec((None, s, LANES), lambda bi, hp, *_: (bi, 0, hp))
    kvspec = pl.BlockSpec((None, LANES, s), lambda bi, hp, *_: (bi, hp, 0))
    return pl.pallas_call(
        functools.partial(_prompt_attn_kernel, bq=bq, bk=bk),
        grid_spec=pltpu.PrefetchScalarGridSpec(
            num_scalar_prefetch=2,
            grid=(b, n_pairs),
            in_specs=[qspec, kvspec, kvspec,
                      pl.BlockSpec((None, None, HEADS_PER_STEP, s), lambda bi, hp, *_: (bi, hp, 0, 0))],
            out_specs=qspec,
            scratch_shapes=[pltpu.VMEM((nq, HEADS_PER_STEP * bq, MXU_DIM), BF16),
                            pltpu.VMEM((MXU_DIM, s), BF16),
                            pltpu.VMEM((HEADS_PER_STEP, LANES, s), BF16),
                            pltpu.VMEM((HEADS_PER_STEP, s, LANES), F32),
                            pltpu.VMEM((HEADS_PER_STEP, s, LANES), F32),
                            pltpu.VMEM((2, HEADS_PER_STEP * bq, bk), F32),
                            pltpu.VMEM((2, HEADS_PER_STEP, bq, bk), BF16),
                            pltpu.VMEM((2, HEADS_PER_STEP, bq, LANES), F32)],
        ),
        out_shape=jax.ShapeDtypeStruct((b, s, ATT_WIDTH), BF16),
        compiler_params=_params(("arbitrary", "arbitrary")),
        name="prompt_attn",
    )(pi, pj, q, ktb, vtb, ct)


def _decay_kernel(pt_ref, lf_hbm, wcat_ref, psuf_ref, o_ref, x_ref, sem, *, n_pool):
    d = pl.program_id(0)
    b = pl.program_id(1)
    nb = pl.num_programs(1)
    step = d * nb + b
    slot = step & 1
    n_pages = x_ref.shape[2]

    def page_copy(p, page, slot_):
        return pltpu.make_async_copy(lf_hbm.at[page], x_ref.at[slot_, :, pl.ds(p, 1), :], sem.at[slot_])

    def fetch(d_, b_, slot_):
        def start(p, carry):
            page_copy(p, pt_ref[b_, p] + d_ * n_pool, slot_).start()
            return carry
        lax.fori_loop(0, n_pages, start, 0, unroll=8)

    @pl.when(step == 0)
    def _():
        fetch(d, b, slot)

    @pl.when(step + 1 < pl.num_programs(0) * nb)
    def _():
        wrap = b + 1 == nb
        fetch(jnp.where(wrap, d + 1, d), jnp.where(wrap, 0, b + 1), 1 - slot)

    def wait(p, carry):
        page_copy(p, 0, slot).wait()
        return carry

    lax.fori_loop(0, n_pages, wait, 0, unroll=8)

    page = x_ref.shape[3]
    xs = _split3(x_ref[slot].reshape(N_HEADS * n_pages, page))
    wc = sum(jnp.dot(t, wcat_ref[...], preferred_element_type=F32) for t in xs)
    head = lambda h: slice(h * n_pages, (h + 1) * n_pages)
    totals = jnp.concatenate([wc[head(h), page:] for h in range(N_HEADS)], axis=1)
    later = sum(jnp.dot(psuf_ref[...], t, preferred_element_type=F32) for t in _split3(totals))
    for h in range(N_HEADS):
        o_ref[h] = wc[head(h), :page] + later[:, h * page:(h + 1) * page]


def _decay_bias(page_table, logf_pages):
    depth, n_pool, n_heads, page = logf_pages.shape
    db, n_pages = page_table.shape
    t = jnp.arange(page)
    tsuf = (t[:, None] > t[None, :]).astype(BF16)
    wcat = jnp.concatenate([tsuf, jnp.ones((page, page), BF16)], axis=1)
    pidx = jnp.arange(n_pages)
    psuf = (pidx[None, :] > pidx[:, None]).astype(BF16)
    const = lambda shape: pl.BlockSpec(shape, lambda d, b, pt: (0,) * len(shape), pipeline_mode=pl.Buffered(1))
    return pl.pallas_call(
        functools.partial(_decay_kernel, n_pool=n_pool),
        grid_spec=pltpu.PrefetchScalarGridSpec(
            num_scalar_prefetch=1,
            grid=(depth, db),
            in_specs=[pl.BlockSpec(memory_space=pl.ANY), const(wcat.shape), const(psuf.shape)],
            out_specs=pl.BlockSpec((None, None, n_heads, n_pages, page), lambda d, b, pt: (d, b, 0, 0, 0)),
            scratch_shapes=[pltpu.VMEM((2, n_heads, n_pages, page), F32), pltpu.SemaphoreType.DMA((2,))],
        ),
        out_shape=jax.ShapeDtypeStruct((depth, db, n_heads, n_pages, page), F32),
        compiler_params=_params(("arbitrary", "arbitrary")),
        name="decay_bias",
    )(page_table, logf_pages.reshape(depth * n_pool, n_heads, 1, page), wcat, psuf)


def _expand_heads(row_of_head, reps):
    return jnp.concatenate([jnp.broadcast_to(row_of_head(h), (reps, row_of_head(h).shape[1]))
                            for h in range(N_HEADS)], axis=0)


def _sample_attn_kernel(pt_ref, q_ref, kt_hbm, vt_hbm, bias_ref, kn_ref, vn_ref, lfn_ref, o_ref,
                        kbuf_ref, vbuf_ref, sem, qbd_ref, acc_ref, m_ref, l_ref, s_ref, kpad_ref, vpad_ref,
                        *, pps, t_new, base, lookahead):
    b = pl.program_id(0)
    j = pl.program_id(1)
    nb = pl.num_programs(0)
    n_steps = pl.num_programs(1) - 1
    n_slots = kbuf_ref.shape[0]
    chunk = b * n_steps + jnp.minimum(j, n_steps - 1)
    slot_of = lambda c: lax.rem(c, n_slots)

    def chunk_copies(bb, jj, slot, known_pages):
        copies = []
        for r in range(pps):
            pg = pt_ref[bb, jj * pps + r] + base if known_pages else 0
            copies.append(pltpu.make_async_copy(kt_hbm.at[pg], kbuf_ref.at[slot, r], sem.at[slot]))
            copies.append(pltpu.make_async_copy(vt_hbm.at[pg], vbuf_ref.at[slot, r], sem.at[slot]))
        return copies

    @pl.when((b == 0) & (j == 0))
    def _():
        for x in range(lookahead):
            for c in chunk_copies(0, x, x, True):
                c.start()

    @pl.when(j < n_steps)
    def _():
        ahead = j + lookahead
        wrap = ahead >= n_steps
        b_ahead = jnp.where(wrap, b + 1, b)
        j_ahead = jnp.where(wrap, ahead - n_steps, ahead)

        @pl.when(b_ahead < nb)
        def _():
            for c in chunk_copies(b_ahead, j_ahead, slot_of(chunk + lookahead), True):
                c.start()

        for c in chunk_copies(b, j, slot_of(chunk), False):
            c.wait()

    slot = slot_of(chunk)
    vslot = jnp.where((j == 0) | (j == n_steps), slot, slot_of(chunk + n_slots - 1))

    @pl.when(j == 0)
    def _():
        qt = jnp.concatenate([q_ref[...].astype(F32)] * N_HEADS, axis=0)
        r = lax.broadcasted_iota(jnp.int32, qt.shape, 0) // t_new
        c = lax.broadcasted_iota(jnp.int32, qt.shape, 1) // HEAD_DIM
        qbd_ref[...] = jnp.where(r == c, qt, 0.0).astype(BF16)
        acc_ref[...] = jnp.zeros_like(acc_ref)
        l_ref[...] = jnp.zeros_like(l_ref)
        m_ref[...] = jnp.full_like(m_ref, NEG_INF)
        s_ref[...] = jnp.full_like(s_ref, NEG_INF)

    def update(scores, weighted_values):
        m_prev = m_ref[...]
        m_cur = scores[0].max(axis=1, keepdims=True)
        for s in scores[1:]:
            m_cur = jnp.maximum(m_cur, s.max(axis=1, keepdims=True))
        m_new = jnp.maximum(m_prev, m_cur)
        alpha = jnp.exp(m_prev - m_new)
        l_new = alpha * l_ref[...]
        acc = alpha[:, :1] * acc_ref[...]
        for s, pv in zip(scores, weighted_values):
            p = jnp.exp(s - m_new[:, :1])
            l_new = l_new + p.sum(axis=1, keepdims=True)
            acc = acc + pv(p.astype(BF16))
        m_ref[...] = m_new
        l_ref[...] = l_new
        acc_ref[...] = acc

    waiting = [s_ref[r] for r in range(pps)]
    qbd = qbd_ref[...]
    for r in range(pps):
        s = jnp.dot(qbd, kbuf_ref[slot, r].astype(BF16), preferred_element_type=F32)
        s_ref[r] = s + _expand_heads(lambda h: bias_ref[h, r:r + 1, :], t_new)
    update(waiting,
           [lambda p, r=r: lax.dot_general(p, vbuf_ref[vslot, r].astype(BF16), NT_DIMS, preferred_element_type=F32)
            for r in range(pps)])

    @pl.when(j == pl.num_programs(1) - 1)
    def _():
        kpad_ref[...] = jnp.zeros_like(kpad_ref)
        vpad_ref[...] = jnp.zeros_like(vpad_ref)
        kpad_ref[:t_new, :] = kn_ref[...]
        vpad_ref[:t_new, :] = vn_ref[...]
        cn = _lane_cumsum(lfn_ref[...])
        s = lax.dot_general(qbd, kpad_ref[...].astype(BF16), NT_DIMS, preferred_element_type=F32)
        s = s - _expand_heads(lambda h: cn[h:h + 1, :], t_new)
        qi = lax.broadcasted_iota(jnp.int32, s.shape, 0) % t_new
        ki = lax.broadcasted_iota(jnp.int32, s.shape, 1)
        s = jnp.where(ki <= qi, s, NEG_INF)
        update([s], [lambda p: jnp.dot(p, vpad_ref[...].astype(BF16), preferred_element_type=F32)])
        out = acc_ref[...] / l_ref[:, :1]
        res = jnp.zeros((t_new, ATT_WIDTH), F32)
        lane_head = lax.broadcasted_iota(jnp.int32, res.shape, 1) // HEAD_DIM
        for h in range(N_HEADS):
            res = jnp.where(lane_head == h, out[h * t_new:(h + 1) * t_new, :], res)
        o_ref[...] = res.astype(o_ref.dtype)


def _sample_attn(page_table, q, k_new, v_new, lf_new, kt_pages, vt_pages, bias, layer, *, pps):
    db, t_new, _ = q.shape
    n_pages = page_table.shape[1]
    page = kt_pages.shape[2]
    n_pool = kt_pages.shape[0] // bias.shape[0]
    base = layer * n_pool
    rows = N_HEADS * t_new
    n_steps = n_pages // pps
    lookahead = min(2, n_steps)
    n_slots = lookahead + 2
    key_step = lambda j: jnp.minimum(j, n_steps - 1)
    tok = pl.BlockSpec((None, t_new, ATT_WIDTH), lambda b, j, pt: (b, 0, 0))
    hbm = pl.BlockSpec(memory_space=pl.ANY)
    in_specs = [tok, hbm, hbm,
                pl.BlockSpec((None, None, N_HEADS, pps, page), lambda b, j, pt: (layer, b, 0, key_step(j), 0)),
                tok, tok, pl.BlockSpec((None, N_HEADS, LANES), lambda b, j, pt: (b, 0, 0))]
    return pl.pallas_call(
        functools.partial(_sample_attn_kernel, pps=pps, t_new=t_new, base=base, lookahead=lookahead),
        grid_spec=pltpu.PrefetchScalarGridSpec(
            num_scalar_prefetch=1,
            grid=(db, n_steps + 1),
            in_specs=in_specs,
            out_specs=tok,
            scratch_shapes=[pltpu.VMEM((n_slots, pps, ATT_WIDTH, page), F32),
                            pltpu.VMEM((n_slots, pps, ATT_WIDTH, page), F32),
                            pltpu.SemaphoreType.DMA((n_slots,)),
                            pltpu.VMEM((rows, ATT_WIDTH), BF16), pltpu.VMEM((rows, ATT_WIDTH), F32),
                            pltpu.VMEM((rows, LANES), F32), pltpu.VMEM((rows, LANES), F32),
                            pltpu.VMEM((pps, rows, page), F32),
                            pltpu.VMEM((page, ATT_WIDTH), F32), pltpu.VMEM((page, ATT_WIDTH), F32)],
        ),
        out_shape=jax.ShapeDtypeStruct((db, t_new, ATT_WIDTH), BF16),
        compiler_params=_params(("arbitrary", "arbitrary")),
        name="sample_attn",
    )(page_table, q, kt_pages, vt_pages, bias, k_new, v_new, lf_new)


def _pool_kernel(u_ref, hist_ref, o_ref, z_ref, *, pos0):
    for b in range(u_ref.shape[0]):
        z_ref[b, :HIST_ROWS, :] = hist_ref[b]
        z_ref[b, HIST_ROWS:, :] = u_ref[b]
        _pooled_rows(z_ref.at[b], o_ref.at[b], pos0)


def _pooled_rows(z_ref, o_ref, pos0):
    tm = o_ref.shape[0]
    row = lax.broadcasted_iota(jnp.int32, (tm, 1), 0)
    pos_count = pos0 + row + 1
    for g, w in enumerate(POOL_WINDOWS):
        cols = slice(g * POOL_GROUP, (g + 1) * POOL_GROUP)
        total = z_ref[:, cols]
        shift = 1
        while shift < w:
            total = total + pltpu.roll(total, shift, axis=0)
            shift *= 2
        cnt = jnp.minimum(pos_count, w).astype(F32)
        o_ref[:, cols] = (total[HIST_ROWS:, :] / cnt - z_ref[HIST_ROWS:, cols]).astype(o_ref.dtype)


def _pool_means(u, hist, *, pos0):
    nb, t, _ = u.shape
    whole = lambda x: pl.BlockSpec(x.shape, lambda i: (0,) * x.ndim)
    return pl.pallas_call(
        functools.partial(_pool_kernel, pos0=pos0),
        grid=(1,),
        in_specs=[whole(u), whole(hist)],
        out_specs=whole(u),
        out_shape=jax.ShapeDtypeStruct(u.shape, BF16),
        scratch_shapes=[pltpu.VMEM((nb, HIST_ROWS + t, POOL_WIDTH), F32)],
        compiler_params=_params(("arbitrary",)),
        name="pool_means",
    )(u, hist)


def _post_kernel(h_ref, att_ref, pooled_ref, p_ref, wpool_ref, pscale_ref, gfox_ref, gpool_ref, wout_ref,
                 gpostmix_ref, gpremlp_ref, wup_ref, wdown_ref, gpostmlp_ref, wple_ref, gple_ref, wgate_ref,
                 o_ref, acc_ref, *, ff_chunk, parts):
    dot = functools.partial(jnp.dot, preferred_element_type=F32)
    n = h_ref.shape[0] // parts
    rows = [slice(k * n, (k + 1) * n) for k in range(parts)]
    groups = range(parts)

    def pool_group_maps(r):
        pooled = pooled_ref[r, :]
        ys = [dot(pooled[:, g * POOL_GROUP:(g + 1) * POOL_GROUP], wpool_ref[g]) for g in range(len(POOL_WINDOWS))]
        return _rms(jnp.concatenate(ys, axis=-1) * pscale_ref[...], gpool_ref[...]).astype(BF16)

    an = [_rms(att_ref[r, :].astype(F32), gfox_ref[...]).astype(BF16) for r in rows]
    pn = [pool_group_maps(r) for r in rows]
    mix = [dot(an[k], wout_ref[:ATT_WIDTH, :]) + dot(pn[k], wout_ref[ATT_WIDTH:, :]) for k in groups]
    h1 = [h_ref[rows[k], :] + _rms(mix[k], gpostmix_ref[...]) for k in groups]
    m = [_rms(h1[k], gpremlp_ref[...]).astype(BF16) for k in groups]
    d_ff = wup_ref.shape[1]
    for c in range(d_ff // ff_chunk):
        cols = slice(c * ff_chunk, (c + 1) * ff_chunk)
        hidden = [jnp.square(jnp.maximum(dot(m[k], wup_ref[:, cols]), 0.0)).astype(BF16) for k in groups]
        for k in groups:
            part = dot(hidden[k], wdown_ref[cols, :])
            if c == 0:
                acc_ref[rows[k], :] = part
            else:
                acc_ref[rows[k], :] += part
    h2 = [h1[k] + _rms(acc_ref[rows[k], :], gpostmlp_ref[...]) for k in groups]
    e = [_rms(dot(p_ref[r, :].astype(BF16), wple_ref[...]), gple_ref[...]) for r in rows]
    gate = [jax.nn.sigmoid(dot(h2[k].astype(BF16), wgate_ref[...])) for k in groups]
    for k in groups:
        o_ref[rows[k], :] = h2[k] + gate[k] * e[k]


def _post(h, att, pooled, p, w, *, tm, ff_chunk):
    n, d = h.shape
    row = lambda width: pl.BlockSpec((tm, width), lambda i: (i, 0))
    weights = (w["w_pool"], w["pool_scale"], w["g_fox"], w["g_pool"], w["w_out"], w["g_post_mix"], w["g_pre_mlp"],
               w["w_up"], w["w_down"], w["g_post_mlp"], w["w_ple"], w["g_ple"], w["w_ple_gate"])
    return pl.pallas_call(
        functools.partial(_post_kernel, ff_chunk=ff_chunk, parts=2 if tm % (2 * MXU_DIM) == 0 else 1),
        grid=(n // tm,),
        in_specs=[row(d), row(ATT_WIDTH), row(POOL_WIDTH), row(p.shape[1])] + [_const_spec(x.shape) for x in weights],
        out_specs=row(d),
        out_shape=jax.ShapeDtypeStruct((n, d), F32),
        scratch_shapes=[pltpu.VMEM((tm, d), F32)],
        compiler_params=_params(("arbitrary",)),
        name="post",
    )(h, att, pooled, p, *weights)


def _tile(n, target):
    t = min(n, target)
    assert n % t == 0, (n, t)
    return t


def kernel(x_prompt, x_sample, cache_k, cache_v, cache_logf, state_pool, page_table, p_prompt, p_sample, w_in, b_f, w_pool, pool_scale, g_fox, g_pool, w_out, g_pre_mix, g_post_mix, g_pre_mlp, g_post_mlp, w_up, w_down, w_ple, g_ple, w_ple_gate):
    b, s, d = x_prompt.shape
    db, t_new, _ = x_sample.shape
    depth, n_pool, page, n_heads, head_dim = cache_k.shape
    n_pages = page_table.shape[1]
    assert (n_heads, head_dim) == (N_HEADS, HEAD_DIM) and w_in.shape[2] == 3 * ATT_WIDTH + POOL_WIDTH + N_HEADS
    assert page == LANES and t_new == 8 and state_pool.shape[2:] == (POOL_BUF, POOL_WIDTH)
    w = ATT_WIDTH
    n_main = 3 * w + POOL_WIDTH
    past = n_pages * page

    w_bf = w_in.astype(BF16)
    w_rows = w_bf[:, :, :n_main]
    w_qu = jnp.concatenate([w_bf[:, :, :w], w_bf[:, :, 3 * w:n_main]], axis=2)
    w_t = jnp.swapaxes(w_bf, 1, 2)
    w_kvt = w_t[:, w:3 * w]
    w_ft = w_t[:, n_main:]
    vec = lambda x: x[:, None, :]
    layers = []
    for i in range(depth):
        layers.append(dict(
            w_pool=w_pool[i].astype(BF16), pool_scale=vec(pool_scale)[i], g_fox=vec(g_fox)[i], g_pool=vec(g_pool)[i],
            w_out=w_out[i].astype(BF16), g_post_mix=vec(g_post_mix)[i], g_pre_mlp=vec(g_pre_mlp)[i],
            w_up=w_up[i].astype(BF16), w_down=w_down[i].astype(BF16), g_post_mlp=vec(g_post_mlp)[i],
            w_ple=w_ple[i].astype(BF16), g_ple=vec(g_ple)[i], w_ple_gate=w_ple_gate[i].astype(BF16)))

    kt_pages = jnp.transpose(cache_k, (0, 1, 3, 4, 2)).reshape(depth * n_pool, w, page)
    vt_pages = jnp.transpose(cache_v, (0, 1, 3, 4, 2)).reshape(depth * n_pool, w, page)
    bias = _decay_bias(page_table, jnp.swapaxes(cache_logf, 2, 3))

    tm_in = _tile(s, 512)
    bq = _tile(s, MXU_DIM)
    bk = _tile(s, 2 * MXU_DIM)
    tm_post = _tile(b * s, 512)
    pps = _tile(n_pages, 8)
    n_s = db * t_new
    tok3 = lambda x: x.reshape(db, t_new, x.shape[-1])

    hp = x_prompt
    hs = x_sample.reshape(1, n_s, d)
    kv_bufs = None
    pool_p, k_s, v_s, lf_s_all, pool_s = [], [], [], [], []
    for i in range(depth):
        g_pre = vec(g_pre_mix)[i]
        bf_col = b_f[i][:, None]
        q, pooled, u_tail, kt_all, vt_all, ktb, vtb, lft_all, ct = _inproj_cols(
            hp, g_pre, w_qu[i], w_kvt[i], w_ft[i], bf_col, i, depth, kv_bufs, tm=tm_in)
        kv_bufs = (kt_all, vt_all, lft_all)
        att = _prompt_attn(q, ktb, vtb, ct, bq=bq, bk=bk)
        hp = _post(hp.reshape(b * s, d), att.reshape(b * s, w), pooled.reshape(b * s, POOL_WIDTH),
                   p_prompt[i].reshape(b * s, -1), layers[i], tm=tm_post, ff_chunk=512).reshape(b, s, d)
        pool_p.append(u_tail[:, HIST_ROWS - POOL_BUF:, :])
        qs, ks, vs, us, lfts = _inproj_rows(hs, g_pre, w_rows[i], w_ft[i], bf_col)
        lf_s = jnp.swapaxes(lfts[0], 0, 1).reshape(db, t_new, N_HEADS)
        lf_new = jnp.pad(jnp.swapaxes(lf_s, 1, 2), ((0, 0), (0, 0), (0, LANES - t_new)))
        att_s = _sample_attn(page_table, tok3(qs[0]), tok3(ks[0]), tok3(vs[0]), lf_new, kt_pages, vt_pages, bias, i,
                             pps=pps)
        hist = jnp.pad(state_pool[i], ((0, 0), (1, 0), (0, 0)))
        pooled_s = _pool_means(tok3(us[0]), hist, pos0=past)
        hs = _post(hs.reshape(n_s, d), att_s.reshape(n_s, w), pooled_s.reshape(n_s, POOL_WIDTH),
                   p_sample[i].reshape(n_s, -1), layers[i], tm=n_s, ff_chunk=512).reshape(1, n_s, d)
        k_s.append(ks.reshape(db, t_new, N_HEADS, HEAD_DIM))
        v_s.append(vs.reshape(db, t_new, N_HEADS, HEAD_DIM))
        lf_s_all.append(lf_s)
        pool_s.append(jnp.concatenate([state_pool[i], tok3(us[0])], axis=1)[:, -POOL_BUF:])
    kt_all, vt_all, lft_all = kv_bufs
    heads_last = lambda xt: jnp.transpose(xt.reshape(depth, b, N_HEADS, HEAD_DIM, s), (0, 1, 4, 2, 3))
    return (hp, hs.reshape(db, t_new, d), heads_last(kt_all), heads_last(vt_all), jnp.swapaxes(lft_all, 2, 3),
            jnp.stack(pool_p), jnp.stack(k_s), jnp.stack(v_s), jnp.stack(lf_s_all), jnp.stack(pool_s))
```

```python
import functools
import math

import jax
import jax.numpy as jnp
import numpy as np
from jax import lax
from jax.experimental import pallas as pl
from jax.experimental.pallas import tpu as pltpu

F32 = jnp.float32
BF16 = jnp.bfloat16

EPS = 1e-6
NEG_INF = -1e30
HEAD_DIM = 64
N_HEADS = 8
ATT_WIDTH = N_HEADS * HEAD_DIM
POOL_WINDOWS = (2, 4, 8, 16)
POOL_GROUP = 128
POOL_WIDTH = POOL_GROUP * len(POOL_WINDOWS)
POOL_BUF = max(POOL_WINDOWS) - 1
HIST_ROWS = POOL_BUF + 1
ATT_SCALE = 1.0 / math.sqrt(HEAD_DIM)

LANES = 128
MXU_DIM = 256
HEADS_PER_STEP = LANES // HEAD_DIM
DMA_GROUP = 8
MAX_TICKS_PER_TRIP = 18
VMEM_LIMIT = 56 * 1024 * 1024

NT_DIMS = (((1,), (1,)), ((), ()))


def _params(semantics):
    return pltpu.CompilerParams(dimension_semantics=semantics, vmem_limit_bytes=VMEM_LIMIT)


def _rms(x, g):
    return x * lax.rsqrt(jnp.mean(x * x, axis=-1, keepdims=True) + EPS) * g


def _const_spec(shape):
    zeros = (0,) * len(shape)
    return pl.BlockSpec(shape, lambda *_: zeros, pipeline_mode=pl.Buffered(1))


def _lane_cumsum(x):
    n = x.shape[1]
    lane = lax.broadcasted_iota(jnp.int32, x.shape, 1)
    shift = 1
    while shift < n:
        x = x + jnp.where(lane >= shift, pltpu.roll(x, shift, axis=1), 0.0)
        shift *= 2
    return x


def _split3(x):
    x1 = x.astype(BF16)
    r1 = x - x1.astype(F32)
    x2 = r1.astype(BF16)
    x3 = (r1 - x2.astype(F32)).astype(BF16)
    return x1, x2, x3


def _inproj_kernel(h_ref, g_ref, w_ref, wt_ref, wft_ref, bf_ref, *refs, transposed_kv, n_alias=0, layer=None):
    refs = refs[n_alias:]

    def put(ref, val):
        if layer is None:
            ref[...] = val
        else:
            for l in range(ref.shape[0]):
                ref[l] = val if l == layer else jnp.zeros_like(val)

    if transposed_kv:
        @pl.when(pl.program_id(1) == 0)
        def _():
            refs[-2][...] = jnp.zeros_like(refs[-2])
            refs[-1][:HIST_ROWS, :] = jnp.zeros((HIST_ROWS, POOL_WIDTH), F32)

    a = _rms(h_ref[...], g_ref[...]).astype(BF16)
    w = ATT_WIDTH
    f = lax.dot_general(wft_ref[...], a, NT_DIMS, preferred_element_type=F32) + bf_ref[...]
    lf = jnp.minimum(f, 0.0) - jnp.log1p(jnp.exp(-jnp.abs(f)))
    if transposed_kv:
        q_ref, pooled_ref, utail_ref, kt_ref, vt_ref, ktb_ref, vtb_ref, lft_ref, ct_ref, carry_ref, z_ref = refs
        tm = q_ref.shape[0]
        put(lft_ref, lf)
        c = _lane_cumsum(lf) + carry_ref[:, :1]
        ct_ref[...] = c
        carry_ref[...] = jnp.broadcast_to(c[:, -1:], carry_ref.shape)
        z = jnp.dot(a, w_ref[...], preferred_element_type=F32)
        q_ref[...] = (z[:, :w] * ATT_SCALE).astype(BF16)
        z_ref[HIST_ROWS:, :] = z[:, w:]
        _pooled_rows(z_ref, pooled_ref, pl.program_id(1) * tm)
        tail = z_ref[tm:, :]
        utail_ref[...] = tail
        z_ref[:HIST_ROWS, :] = tail
        zt = lax.dot_general(wt_ref[...], a, NT_DIMS, preferred_element_type=F32)
        put(kt_ref, zt[:w])
        put(vt_ref, zt[w:])
        ktb_ref[...] = zt[:w].astype(BF16)
        vtb_ref[...] = zt[w:].astype(BF16)
    else:
        z = jnp.dot(a, w_ref[...], preferred_element_type=F32)
        q_ref, k_ref, v_ref, u_ref, lft_ref = refs
        q_ref[...] = (z[:, :w] * ATT_SCALE).astype(BF16)
        k_ref[...] = z[:, w:2 * w]
        v_ref[...] = z[:, 2 * w:3 * w]
        u_ref[...] = z[:, 3 * w:]
        lft_ref[...] = lf


def _inproj_rows(h, g, w_rows, w_ft, b_f):
    nb, s, d = h.shape
    row = lambda width: pl.BlockSpec((None, s, width), lambda b, i: (b, 0, 0))
    sds = jax.ShapeDtypeStruct
    return pl.pallas_call(
        functools.partial(_inproj_kernel, transposed_kv=False, n_alias=0),
        grid=(nb, 1),
        in_specs=[row(d), _const_spec((1, d)), _const_spec(w_rows.shape), _const_spec(w_ft.shape),
                  _const_spec((N_HEADS, d)), _const_spec((N_HEADS, 1))],
        out_specs=[row(ATT_WIDTH)] * 3 + [row(POOL_WIDTH), pl.BlockSpec((None, N_HEADS, s), lambda b, i: (b, 0, 0))],
        out_shape=[sds((nb, s, ATT_WIDTH), BF16), sds((nb, s, ATT_WIDTH), F32), sds((nb, s, ATT_WIDTH), F32),
                   sds((nb, s, POOL_WIDTH), F32), sds((nb, N_HEADS, s), F32)],
        compiler_params=_params(("arbitrary", "arbitrary")),
        name="inproj_rows",
    )(h, g, w_rows, w_ft, w_ft, b_f)


def _inproj_cols(h, g, w_qu, w_kvt, w_ft, b_f, layer, depth, kv_prev, *, tm):
    nb, s, d = h.shape
    row = lambda width: pl.BlockSpec((None, tm, width), lambda b, i: (b, i, 0))
    col = lambda height: pl.BlockSpec((None, height, tm), lambda b, i: (b, 0, i))
    if kv_prev is None:
        layer_col = lambda height: pl.BlockSpec((depth, None, height, tm), lambda b, i: (0, b, 0, i))
    else:
        layer_col = lambda height: pl.BlockSpec((None, None, height, tm), lambda b, i: (layer, b, 0, i))
    sds = jax.ShapeDtypeStruct
    out_shape = [sds((nb, s, ATT_WIDTH), BF16), sds((nb, s, POOL_WIDTH), BF16), sds((nb, HIST_ROWS, POOL_WIDTH), F32),
                 sds((depth, nb, ATT_WIDTH, s), F32), sds((depth, nb, ATT_WIDTH, s), F32),
                 sds((nb, ATT_WIDTH, s), BF16), sds((nb, ATT_WIDTH, s), BF16),
                 sds((depth, nb, N_HEADS, s), F32), sds((nb, N_HEADS, s), F32)]
    out_specs = [row(ATT_WIDTH), row(POOL_WIDTH), pl.BlockSpec((None, HIST_ROWS, POOL_WIDTH), lambda b, i: (b, 0, 0)),
                 layer_col(ATT_WIDTH), layer_col(ATT_WIDTH),
                 col(ATT_WIDTH), col(ATT_WIDTH), layer_col(N_HEADS), col(N_HEADS)]
    in_specs = [row(d), _const_spec((1, d)), _const_spec(w_qu.shape), _const_spec(w_kvt.shape),
                _const_spec((N_HEADS, d)), _const_spec((N_HEADS, 1))]
    args = [h, g, w_qu, w_kvt, w_ft, b_f]
    aliases = {}
    if kv_prev is not None:
        for out_idx, buf in zip((3, 4, 7), kv_prev):
            aliases[len(args)] = out_idx
            in_specs.append(pl.BlockSpec(memory_space=pl.ANY))
            args.append(buf)
    return pl.pallas_call(
        functools.partial(_inproj_kernel, transposed_kv=True, n_alias=len(aliases),
                          layer=layer if kv_prev is None else None),
        grid=(nb, s // tm),
        in_specs=in_specs,
        out_specs=out_specs,
        out_shape=out_shape,
        scratch_shapes=[pltpu.VMEM((N_HEADS, LANES), F32), pltpu.VMEM((HIST_ROWS + tm, POOL_WIDTH), F32)],
        input_output_aliases=aliases,
        compiler_params=_params(("arbitrary", "arbitrary")),
        name="inproj_cols",
    )(*args)


def _prompt_attn_kernel(pi_ref, pj_ref, q_ref, kt_ref, vt_ref, c_ref, o_ref,
                        qs_ref, kta_ref, vta_ref, acc_ref, m_ref, s_ref, p_ref, alpha_ref, *, bq, bk):
    s_len = q_ref.shape[0]
    nq = s_len // bq
    per = bk // bq
    first_lanes = lax.broadcasted_iota(jnp.int32, (1, LANES), 1) < HEAD_DIM
    first_rows = lax.broadcasted_iota(jnp.int32, (LANES, 1), 0) < HEAD_DIM

    q2 = q_ref[...].reshape(nq, bq, LANES)
    zero = jnp.zeros_like(q2)
    lane = lax.broadcasted_iota(jnp.int32, (bq, LANES), 1)
    ones_a = jnp.where(lane < 3, 1.0, 0.0).astype(BF16)
    ones_b = jnp.where((lane >= 3) & (lane < 6), 1.0, 0.0).astype(BF16)
    first_q = (lane < HEAD_DIM)[None]
    qs_ref[:, :bq, :LANES] = jnp.where(first_q, q2, zero)
    qs_ref[:, bq:, :LANES] = jnp.where(first_q, zero, q2)

    @pl.when((pl.program_id(0) == 0) & (pl.program_id(1) == 0))
    def _():
        qs_ref[:, :bq, LANES:] = jnp.broadcast_to(ones_a[None], (nq, bq, LANES))
        qs_ref[:, bq:, LANES:] = jnp.broadcast_to(ones_b[None], (nq, bq, LANES))
        kta_ref[LANES + 16:, :] = jnp.zeros((LANES - 16, s_len), BF16)

    kta_ref[:LANES, :] = kt_ref[...]
    neg_c = -c_ref[...]
    parts = _split3(neg_c[0:1, :]) + _split3(neg_c[1:2, :])
    r16 = lax.broadcasted_iota(jnp.int32, (16, s_len), 0)
    extra = jnp.zeros((16, s_len), F32)
    for idx, part in enumerate(parts):
        extra = jnp.where(r16 == idx, part.astype(F32), extra)
    kta_ref[LANES:LANES + 16, :] = extra.astype(BF16)
    vt = vt_ref[...]
    one = jnp.ones_like(vt)
    vta_ref[0] = jnp.where(first_rows, vt, one)
    vta_ref[1] = jnp.where(first_rows, one, vt)

    def scores(i, j, slot, masked):
        cols = pl.ds(pl.multiple_of(j * bk, bk), bk)
        s2 = jnp.dot(qs_ref[i], kta_ref[:, cols], preferred_element_type=F32)
        if masked:
            r = (lax.broadcasted_iota(jnp.int32, s2.shape, 0) & (bq - 1)) + (i * bq - j * bk)
            c = lax.broadcasted_iota(jnp.int32, s2.shape, 1)
            s2 = jnp.where(c <= r, s2, NEG_INF)
        s_ref[slot] = s2

    def softmax(i, slot, first):
        rows = pl.ds(pl.multiple_of(i * bq, bq), bq)
        for hh in range(HEADS_PER_STEP):
            head = slice(hh * bq, (hh + 1) * bq)
            m_new = jnp.broadcast_to(jnp.max(s_ref[slot, head, :], axis=1, keepdims=True), (bq, LANES))
            if not first:
                m_prev = m_ref[hh, rows, :]
                m_new = jnp.maximum(m_prev, m_new)
                alpha_ref[slot, hh] = jnp.exp(m_prev - m_new)
            p_ref[slot, hh] = jnp.concatenate(
                [jnp.exp(s_ref[slot, head, k * LANES:(k + 1) * LANES] - m_new) for k in range(bk // LANES)],
                axis=1).astype(BF16)
            m_ref[hh, rows, :] = m_new

    def accumulate(i, j, slot, first):
        rows = pl.ds(pl.multiple_of(i * bq, bq), bq)
        cols = pl.ds(pl.multiple_of(j * bk, bk), bk)
        for hh in range(HEADS_PER_STEP):
            pv = lax.dot_general(p_ref[slot, hh], vta_ref[hh, :, cols], NT_DIMS, preferred_element_type=F32)
            acc_ref[hh, rows, :] = pv if first else acc_ref[hh, rows, :] * alpha_ref[slot, hh] + pv

    def sweep(n, pair, masked):
        first = masked
        if n == 0:
            return
        scores(*pair(0), 0, masked)
        if n == 1:
            softmax(pair(0)[0], 0, first)
            accumulate(*pair(0), 0, first)
            return
        softmax(pair(0)[0], 0, first)
        scores(*pair(1), 1, masked)

        def tick(t, slot):
            accumulate(*pair(t - 2), slot, first)
            softmax(pair(t - 1)[0], 1 - slot, first)
            scores(*pair(t), slot, masked)

        per_trip = max([x for x in range(2, MAX_TICKS_PER_TRIP + 1, 2) if (n - 2) % x == 0], default=2)

        def ticks(k, carry):
            for x in range(per_trip):
                tick(per_trip * k + 2 + x, x % 2)
            return carry

        trips = (n - 2) // per_trip
        lax.fori_loop(0, trips, ticks, 0)
        for t in range(2 + trips * per_trip, n):
            tick(t, t % 2)
        accumulate(*pair(n - 2), n % 2, first)
        softmax(pair(n - 1)[0], (n - 1) % 2, first)
        accumulate(*pair(n - 1), (n - 1) % 2, first)

    def last_key_block(i):
        return i // per if isinstance(i, int) else lax.shift_right_logical(i, per.bit_length() - 1)

    sweep(nq, lambda t: (t, last_key_block(t)), True)
    sweep(sum(i // per for i in range(nq)), lambda t: (pi_ref[t], pj_ref[t]), False)

    def finish(i, carry):
        rows = pl.ds(pl.multiple_of(i * bq, bq), bq)
        a = acc_ref[0, rows, :]
        b = acc_ref[1, rows, :]
        sums = pltpu.roll(jnp.where(first_lanes, b, a), HEAD_DIM, axis=1)
        o_ref[rows, :] = (jnp.where(first_lanes, a, b) / sums).astype(o_ref.dtype)
        return carry

    lax.fori_loop(0, nq, finish, 0, unroll=4)


def _prompt_attn(q, ktb, vtb, ct, *, bq, bk):
    b, s, _ = q.shape
    nq = s // bq
    per = bk // bq
    assert s % bk == 0 and bk % bq == 0 and per & (per - 1) == 0 and bq & (bq - 1) == 0
    n_pairs = N_HEADS // HEADS_PER_STEP
    ct = ct.reshape(b, n_pairs, HEADS_PER_STEP, s)
    below = [(i, j) for i in range(nq) for j in range(i // per)] or [(0, 0)]
    pi = jnp.asarray(np.array([p[0] for p in below], np.int32))
    pj = jnp.asarray(np.array([p[1] for p in below], np.int32))
    qspec = pl.BlockSpec((None, s, LANES), lambda bi, hp, *_: (bi, 0, hp))
    kvspec = pl.BlockSpec((None, LANES, s), lambda bi, hp, *_: (bi, hp, 0))
    return pl.pallas_call(
        functools.partial(_prompt_attn_kernel, bq=bq, bk=bk),
        grid_spec=pltpu.PrefetchScalarGridSpec(
            num_scalar_prefetch=2,
            grid=(b, n_pairs),
            in_specs=[qspec, kvspec, kvspec,
                      pl.BlockSpec((None, None, HEADS_PER_STEP, s), lambda bi, hp, *_: (bi, hp, 0, 0))],
            out_specs=qspec,
            scratch_shapes=[pltpu.VMEM((nq, HEADS_PER_STEP * bq, MXU_DIM), BF16),
                            pltpu.VMEM((MXU_DIM, s), BF16),
                            pltpu.VMEM((HEADS_PER_STEP, LANES, s), BF16),
                            pltpu.VMEM((HEADS_PER_STEP, s, LANES), F32),
                            pltpu.VMEM((HEADS_PER_STEP, s, LANES), F32),
                            pltpu.VMEM((2, HEADS_PER_STEP * bq, bk), F32),
                            pltpu.VMEM((2, HEADS_PER_STEP, bq, bk), BF16),
                            pltpu.VMEM((2, HEADS_PER_STEP, bq, LANES), F32)],
        ),
        out_shape=jax.ShapeDtypeStruct((b, s, ATT_WIDTH), BF16),
        compiler_params=_params(("arbitrary", "arbitrary")),
        name="prompt_attn",
    )(pi, pj, q, ktb, vtb, ct)


def _decay_kernel(pt_ref, lf_hbm, wcat_ref, psuf_ref, o_ref, x_ref, sem, *, n_pool):
    d = pl.program_id(0)
    b = pl.program_id(1)
    nb = pl.num_programs(1)
    step = d * nb + b
    slot = step & 1
    n_pages = x_ref.shape[2]

    def page_copy(p, page, slot_):
        return pltpu.make_async_copy(lf_hbm.at[page], x_ref.at[slot_, :, pl.ds(p, 1), :], sem.at[slot_])

    def fetch(d_, b_, slot_):
        def start(g, carry):
            for x in range(DMA_GROUP):
                p = g * DMA_GROUP + x
                page_copy(p, pt_ref[b_, p] + d_ * n_pool, slot_).start(priority=x % 2)
            return carry
        lax.fori_loop(0, n_pages // DMA_GROUP, start, 0)

    @pl.when(step == 0)
    def _():
        fetch(d, b, slot)

    @pl.when(step + 1 < pl.num_programs(0) * nb)
    def _():
        wrap = b + 1 == nb
        fetch(jnp.where(wrap, d + 1, d), jnp.where(wrap, 0, b + 1), 1 - slot)

    def wait(p, carry):
        page_copy(p, 0, slot).wait()
        return carry

    lax.fori_loop(0, n_pages, wait, 0, unroll=8)

    page = x_ref.shape[3]
    xs = _split3(x_ref[slot].reshape(N_HEADS * n_pages, page))
    wc = sum(jnp.dot(t, wcat_ref[...], preferred_element_type=F32) for t in xs)
    head = lambda h: slice(h * n_pages, (h + 1) * n_pages)
    totals = jnp.concatenate([wc[head(h), page:] for h in range(N_HEADS)], axis=1)
    later = sum(jnp.dot(psuf_ref[...], t, preferred_element_type=F32) for t in _split3(totals))
    for h in range(N_HEADS):
        o_ref[h] = wc[head(h), :page] + later[:, h * page:(h + 1) * page]


def _decay_bias(page_table, logf_pages):
    depth, n_pool, n_heads, page = logf_pages.shape
    db, n_pages = page_table.shape
    t = jnp.arange(page)
    tsuf = (t[:, None] > t[None, :]).astype(BF16)
    wcat = jnp.concatenate([tsuf, jnp.ones((page, page), BF16)], axis=1)
    pidx = jnp.arange(n_pages)
    psuf = (pidx[None, :] > pidx[:, None]).astype(BF16)
    const = lambda shape: pl.BlockSpec(shape, lambda d, b, pt: (0,) * len(shape), pipeline_mode=pl.Buffered(1))
    return pl.pallas_call(
        functools.partial(_decay_kernel, n_pool=n_pool),
        grid_spec=pltpu.PrefetchScalarGridSpec(
            num_scalar_prefetch=1,
            grid=(depth, db),
            in_specs=[pl.BlockSpec(memory_space=pl.ANY), const(wcat.shape), const(psuf.shape)],
            out_specs=pl.BlockSpec((None, None, n_heads, n_pages, page), lambda d, b, pt: (d, b, 0, 0, 0)),
            scratch_shapes=[pltpu.VMEM((2, n_heads, n_pages, page), F32), pltpu.SemaphoreType.DMA((2,))],
        ),
        out_shape=jax.ShapeDtypeStruct((depth, db, n_heads, n_pages, page), F32),
        compiler_params=_params(("arbitrary", "arbitrary")),
        name="decay_bias",
    )(page_table, logf_pages.reshape(depth * n_pool, n_heads, 1, page), wcat, psuf)


def _expand_heads(row_of_head, reps):
    return jnp.concatenate([jnp.broadcast_to(row_of_head(h), (reps, row_of_head(h).shape[1]))
                            for h in range(N_HEADS)], axis=0)


def _sample_attn_kernel(pt_ref, q_ref, kt_hbm, vt_hbm, bias_ref, kn_ref, vn_ref, lfn_ref, o_ref,
                        kbuf_ref, vbuf_ref, sem, qbd_ref, acc_ref, m_ref, l_ref, s_ref, kpad_ref, vpad_ref,
                        *, pps, t_new, base, lookahead):
    b = pl.program_id(0)
    j = pl.program_id(1)
    nb = pl.num_programs(0)
    n_steps = pl.num_programs(1) - 1
    n_slots = kbuf_ref.shape[0]
    chunk = b * n_steps + jnp.minimum(j, n_steps - 1)
    slot_of = lambda c: lax.rem(c, n_slots)

    def chunk_copies(bb, jj, slot, known_pages):
        copies = []
        for r in range(pps):
            pg = pt_ref[bb, jj * pps + r] + base if known_pages else 0
            copies.append(pltpu.make_async_copy(kt_hbm.at[pg], kbuf_ref.at[slot, r], sem.at[slot]))
            copies.append(pltpu.make_async_copy(vt_hbm.at[pg], vbuf_ref.at[slot, r], sem.at[slot]))
        return copies

    @pl.when((b == 0) & (j == 0))
    def _():
        for x in range(lookahead):
            for c in chunk_copies(0, x, x, True):
                c.start()

    @pl.when(j < n_steps)
    def _():
        ahead = j + lookahead
        wrap = ahead >= n_steps
        b_ahead = jnp.where(wrap, b + 1, b)
        j_ahead = jnp.where(wrap, ahead - n_steps, ahead)

        @pl.when(b_ahead < nb)
        def _():
            for c in chunk_copies(b_ahead, j_ahead, slot_of(chunk + lookahead), True):
                c.start()

        for c in chunk_copies(b, j, slot_of(chunk), False):
            c.wait()

    slot = slot_of(chunk)
    vslot = jnp.where((j == 0) | (j == n_steps), slot, slot_of(chunk + n_slots - 1))

    @pl.when(j == 0)
    def _():
        qt = jnp.concatenate([q_ref[...].astype(F32)] * N_HEADS, axis=0)
        r = lax.broadcasted_iota(jnp.int32, qt.shape, 0) // t_new
        c = lax.broadcasted_iota(jnp.int32, qt.shape, 1) // HEAD_DIM
        qbd_ref[...] = jnp.where(r == c, qt, 0.0).astype(BF16)
        acc_ref[...] = jnp.zeros_like(acc_ref)
        l_ref[...] = jnp.zeros_like(l_ref)
        m_ref[...] = jnp.full_like(m_ref, NEG_INF)
        s_ref[...] = jnp.full_like(s_ref, NEG_INF)

    def update(scores, weighted_values):
        m_prev = m_ref[...]
        m_cur = scores[0].max(axis=1, keepdims=True)
        for s in scores[1:]:
            m_cur = jnp.maximum(m_cur, s.max(axis=1, keepdims=True))
        m_new = jnp.maximum(m_prev, m_cur)
        alpha = jnp.exp(m_prev - m_new)
        l_new = alpha * l_ref[...]
        acc = alpha[:, :1] * acc_ref[...]
        for s, pv in zip(scores, weighted_values):
            p = jnp.exp(s - m_new[:, :1])
            l_new = l_new + p.sum(axis=1, keepdims=True)
            acc = acc + pv(p.astype(BF16))
        m_ref[...] = m_new
        l_ref[...] = l_new
        acc_ref[...] = acc

    waiting = [s_ref[r] for r in range(pps)]
    qbd = qbd_ref[...]
    for r in range(pps):
        s = jnp.dot(qbd, kbuf_ref[slot, r].astype(BF16), preferred_element_type=F32)
        s_ref[r] = s + _expand_heads(lambda h: bias_ref[h, r:r + 1, :], t_new)
    update(waiting,
           [lambda p, r=r: lax.dot_general(p, vbuf_ref[vslot, r].astype(BF16), NT_DIMS, preferred_element_type=F32)
            for r in range(pps)])

    @pl.when(j == pl.num_programs(1) - 1)
    def _():
        kpad_ref[...] = jnp.zeros_like(kpad_ref)
        vpad_ref[...] = jnp.zeros_like(vpad_ref)
        kpad_ref[:t_new, :] = kn_ref[...]
        vpad_ref[:t_new, :] = vn_ref[...]
        cn = _lane_cumsum(lfn_ref[...])
        s = lax.dot_general(qbd, kpad_ref[...].astype(BF16), NT_DIMS, preferred_element_type=F32)
        s = s - _expand_heads(lambda h: cn[h:h + 1, :], t_new)
        qi = lax.broadcasted_iota(jnp.int32, s.shape, 0) % t_new
        ki = lax.broadcasted_iota(jnp.int32, s.shape, 1)
        s = jnp.where(ki <= qi, s, NEG_INF)
        update([s], [lambda p: jnp.dot(p, vpad_ref[...].astype(BF16), preferred_element_type=F32)])
        out = acc_ref[...] / l_ref[:, :1]
        res = jnp.zeros((t_new, ATT_WIDTH), F32)
        lane_head = lax.broadcasted_iota(jnp.int32, res.shape, 1) // HEAD_DIM
        for h in range(N_HEADS):
            res = jnp.where(lane_head == h, out[h * t_new:(h + 1) * t_new, :], res)
        o_ref[...] = res.astype(o_ref.dtype)


def _sample_attn(page_table, q, k_new, v_new, lf_new, kt_pages, vt_pages, bias, layer, *, pps):
    db, t_new, _ = q.shape
    n_pages = page_table.shape[1]
    page = kt_pages.shape[2]
    n_pool = kt_pages.shape[0] // bias.shape[0]
    base = layer * n_pool
    rows = N_HEADS * t_new
    n_steps = n_pages // pps
    lookahead = min(2, n_steps)
    n_slots = lookahead + 2
    key_step = lambda j: jnp.minimum(j, n_steps - 1)
    tok = pl.BlockSpec((None, t_new, ATT_WIDTH), lambda b, j, pt: (b, 0, 0))
    hbm = pl.BlockSpec(memory_space=pl.ANY)
    in_specs = [tok, hbm, hbm,
                pl.BlockSpec((None, None, N_HEADS, pps, page), lambda b, j, pt: (layer, b, 0, key_step(j), 0)),
                tok, tok, pl.BlockSpec((None, N_HEADS, LANES), lambda b, j, pt: (b, 0, 0))]
    return pl.pallas_call(
        functools.partial(_sample_attn_kernel, pps=pps, t_new=t_new, base=base, lookahead=lookahead),
        grid_spec=pltpu.PrefetchScalarGridSpec(
            num_scalar_prefetch=1,
            grid=(db, n_steps + 1),
            in_specs=in_specs,
            out_specs=tok,
            scratch_shapes=[pltpu.VMEM((n_slots, pps, ATT_WIDTH, page), F32),
                            pltpu.VMEM((n_slots, pps, ATT_WIDTH, page), F32),
                            pltpu.SemaphoreType.DMA((n_slots,)),
                            pltpu.VMEM((rows, ATT_WIDTH), BF16), pltpu.VMEM((rows, ATT_WIDTH), F32),
                            pltpu.VMEM((rows, LANES), F32), pltpu.VMEM((rows, LANES), F32),
                            pltpu.VMEM((pps, rows, page), F32),
                            pltpu.VMEM((page, ATT_WIDTH), F32), pltpu.VMEM((page, ATT_WIDTH), F32)],
        ),
        out_shape=jax.ShapeDtypeStruct((db, t_new, ATT_WIDTH), BF16),
        compiler_params=_params(("arbitrary", "arbitrary")),
        name="sample_attn",
    )(page_table, q, kt_pages, vt_pages, bias, k_new, v_new, lf_new)


def _pool_kernel(u_ref, hist_ref, o_ref, z_ref, *, pos0):
    for b in range(u_ref.shape[0]):
        z_ref[b, :HIST_ROWS, :] = hist_ref[b]
        z_ref[b, HIST_ROWS:, :] = u_ref[b]
        _pooled_rows(z_ref.at[b], o_ref.at[b], pos0)


def _pooled_rows(z_ref, o_ref, pos0):
    tm = o_ref.shape[0]
    row = lax.broadcasted_iota(jnp.int32, (tm, 1), 0)
    pos_count = pos0 + row + 1
    for g, w in enumerate(POOL_WINDOWS):
        cols = slice(g * POOL_GROUP, (g + 1) * POOL_GROUP)
        total = z_ref[:, cols]
        shift = 1
        while shift < w:
            total = total + pltpu.roll(total, shift, axis=0)
            shift *= 2
        cnt = jnp.minimum(pos_count, w).astype(F32)
        o_ref[:, cols] = (total[HIST_ROWS:, :] / cnt - z_ref[HIST_ROWS:, cols]).astype(o_ref.dtype)


def _pool_means(u, hist, *, pos0):
    nb, t, _ = u.shape
    whole = lambda x: pl.BlockSpec(x.shape, lambda i: (0,) * x.ndim)
    return pl.pallas_call(
        functools.partial(_pool_kernel, pos0=pos0),
        grid=(1,),
        in_specs=[whole(u), whole(hist)],
        out_specs=whole(u),
        out_shape=jax.ShapeDtypeStruct(u.shape, BF16),
        scratch_shapes=[pltpu.VMEM((nb, HIST_ROWS + t, POOL_WIDTH), F32)],
        compiler_params=_params(("arbitrary",)),
        name="pool_means",
    )(u, hist)


def _post_kernel(h_ref, att_ref, pooled_ref, p_ref, wpool_ref, pscale_ref, gfox_ref, gpool_ref, wout_ref,
                 gpostmix_ref, gpremlp_ref, wup_ref, wdown_ref, gpostmlp_ref, wple_ref, gple_ref, wgate_ref,
                 o_ref, acc_ref, *, ff_chunk, parts):
    dot = functools.partial(jnp.dot, preferred_element_type=F32)
    n = h_ref.shape[0] // parts
    rows = [slice(k * n, (k + 1) * n) for k in range(parts)]
    groups = range(parts)

    def pool_group_maps(r):
        pooled = pooled_ref[r, :]
        ys = [dot(pooled[:, g * POOL_GROUP:(g + 1) * POOL_GROUP], wpool_ref[g]) for g in range(len(POOL_WINDOWS))]
        return _rms(jnp.concatenate(ys, axis=-1) * pscale_ref[...], gpool_ref[...]).astype(BF16)

    an = [_rms(att_ref[r, :].astype(F32), gfox_ref[...]).astype(BF16) for r in rows]
    pn = [pool_group_maps(r) for r in rows]
    mix = [dot(an[k], wout_ref[:ATT_WIDTH, :]) + dot(pn[k], wout_ref[ATT_WIDTH:, :]) for k in groups]
    h1 = [h_ref[rows[k], :] + _rms(mix[k], gpostmix_ref[...]) for k in groups]
    m = [_rms(h1[k], gpremlp_ref[...]).astype(BF16) for k in groups]
    d_ff = wup_ref.shape[1]
    for c in range(d_ff // ff_chunk):
        cols = slice(c * ff_chunk, (c + 1) * ff_chunk)
        hidden = [jnp.square(jnp.maximum(dot(m[k], wup_ref[:, cols]), 0.0)).astype(BF16) for k in groups]
        for k in groups:
            part = dot(hidden[k], wdown_ref[cols, :])
            if c == 0:
                acc_ref[rows[k], :] = part
            else:
                acc_ref[rows[k], :] += part
    h2 = [h1[k] + _rms(acc_ref[rows[k], :], gpostmlp_ref[...]) for k in groups]
    e = [_rms(dot(p_ref[r, :].astype(BF16), wple_ref[...]), gple_ref[...]) for r in rows]
    gate = [jax.nn.sigmoid(dot(h2[k].astype(BF16), wgate_ref[...])) for k in groups]
    for k in groups:
        o_ref[rows[k], :] = h2[k] + gate[k] * e[k]


def _post(h, att, pooled, p, w, *, tm, ff_chunk):
    n, d = h.shape
    row = lambda width: pl.BlockSpec((tm, width), lambda i: (i, 0))
    weights = (w["w_pool"], w["pool_scale"], w["g_fox"], w["g_pool"], w["w_out"], w["g_post_mix"], w["g_pre_mlp"],
               w["w_up"], w["w_down"], w["g_post_mlp"], w["w_ple"], w["g_ple"], w["w_ple_gate"])
    return pl.pallas_call(
        functools.partial(_post_kernel, ff_chunk=ff_chunk, parts=2 if tm % (2 * MXU_DIM) == 0 else 1),
        grid=(n // tm,),
        in_specs=[row(d), row(ATT_WIDTH), row(POOL_WIDTH), row(p.shape[1])] + [_const_spec(x.shape) for x in weights],
        out_specs=row(d),
        out_shape=jax.ShapeDtypeStruct((n, d), F32),
        scratch_shapes=[pltpu.VMEM((tm, d), F32)],
        compiler_params=_params(("arbitrary",)),
        name="post",
    )(h, att, pooled, p, *weights)


def _tile(n, target):
    t = min(n, target)
    assert n % t == 0, (n, t)
    return t


def kernel(x_prompt, x_sample, cache_k, cache_v, cache_logf, state_pool, page_table, p_prompt, p_sample, w_in, b_f, w_pool, pool_scale, g_fox, g_pool, w_out, g_pre_mix, g_post_mix, g_pre_mlp, g_post_mlp, w_up, w_down, w_ple, g_ple, w_ple_gate):
    b, s, d = x_prompt.shape
    db, t_new, _ = x_sample.shape
    depth, n_pool, page, n_heads, head_dim = cache_k.shape
    n_pages = page_table.shape[1]
    assert (n_heads, head_dim) == (N_HEADS, HEAD_DIM) and w_in.shape[2] == 3 * ATT_WIDTH + POOL_WIDTH + N_HEADS
    assert page == LANES and t_new == 8 and state_pool.shape[2:] == (POOL_BUF, POOL_WIDTH)
    w = ATT_WIDTH
    n_main = 3 * w + POOL_WIDTH
    past = n_pages * page

    w_bf = w_in.astype(BF16)
    w_rows = w_bf[:, :, :n_main]
    w_qu = jnp.concatenate([w_bf[:, :, :w], w_bf[:, :, 3 * w:n_main]], axis=2)
    w_t = jnp.swapaxes(w_bf, 1, 2)
    w_kvt = w_t[:, w:3 * w]
    w_ft = w_t[:, n_main:]
    vec = lambda x: x[:, None, :]
    layers = []
    for i in range(depth):
        layers.append(dict(
            w_pool=w_pool[i].astype(BF16), pool_scale=vec(pool_scale)[i], g_fox=vec(g_fox)[i], g_pool=vec(g_pool)[i],
            w_out=w_out[i].astype(BF16), g_post_mix=vec(g_post_mix)[i], g_pre_mlp=vec(g_pre_mlp)[i],
            w_up=w_up[i].astype(BF16), w_down=w_down[i].astype(BF16), g_post_mlp=vec(g_post_mlp)[i],
            w_ple=w_ple[i].astype(BF16), g_ple=vec(g_ple)[i], w_ple_gate=w_ple_gate[i].astype(BF16)))

    kt_pages = jnp.transpose(cache_k, (0, 1, 3, 4, 2)).reshape(depth * n_pool, w, page)
    vt_pages = jnp.transpose(cache_v, (0, 1, 3, 4, 2)).reshape(depth * n_pool, w, page)
    bias = _decay_bias(page_table, jnp.swapaxes(cache_logf, 2, 3))

    tm_in = _tile(s, 512)
    bq = _tile(s, MXU_DIM)
    bk = _tile(s, 2 * MXU_DIM)
    tm_post = _tile(b * s, 512)
    pps = _tile(n_pages, 8)
    n_s = db * t_new
    tok3 = lambda x: x.reshape(db, t_new, x.shape[-1])

    hp = x_prompt
    hs = x_sample.reshape(1, n_s, d)
    kv_bufs = None
    pool_p, k_s, v_s, lf_s_all, pool_s = [], [], [], [], []
    for i in range(depth):
        g_pre = vec(g_pre_mix)[i]
        bf_col = b_f[i][:, None]
        q, pooled, u_tail, kt_all, vt_all, ktb, vtb, lft_all, ct = _inproj_cols(
            hp, g_pre, w_qu[i], w_kvt[i], w_ft[i], bf_col, i, depth, kv_bufs, tm=tm_in)
        kv_bufs = (kt_all, vt_all, lft_all)
        att = _prompt_attn(q, ktb, vtb, ct, bq=bq, bk=bk)
        hp = _post(hp.reshape(b * s, d), att.reshape(b * s, w), pooled.reshape(b * s, POOL_WIDTH),
                   p_prompt[i].reshape(b * s, -1), layers[i], tm=tm_post, ff_chunk=512).reshape(b, s, d)
        pool_p.append(u_tail[:, HIST_ROWS - POOL_BUF:, :])
        qs, ks, vs, us, lfts = _inproj_rows(hs, g_pre, w_rows[i], w_ft[i], bf_col)
        lf_s = jnp.swapaxes(lfts[0], 0, 1).reshape(db, t_new, N_HEADS)
        lf_new = jnp.pad(jnp.swapaxes(lf_s, 1, 2), ((0, 0), (0, 0), (0, LANES - t_new)))
        att_s = _sample_attn(page_table, tok3(qs[0]), tok3(ks[0]), tok3(vs[0]), lf_new, kt_pages, vt_pages, bias, i,
                             pps=pps)
        hist = jnp.pad(state_pool[i], ((0, 0), (1, 0), (0, 0)))
        pooled_s = _pool_means(tok3(us[0]), hist, pos0=past)
        hs = _post(hs.reshape(n_s, d), att_s.reshape(n_s, w), pooled_s.reshape(n_s, POOL_WIDTH),
                   p_sample[i].reshape(n_s, -1), layers[i], tm=n_s, ff_chunk=512).reshape(1, n_s, d)
        k_s.append(ks.reshape(db, t_new, N_HEADS, HEAD_DIM))
        v_s.append(vs.reshape(db, t_new, N_HEADS, HEAD_DIM))
        lf_s_all.append(lf_s)
        pool_s.append(jnp.concatenate([state_pool[i], tok3(us[0])], axis=1)[:, -POOL_BUF:])
    kt_all, vt_all, lft_all = kv_bufs
    heads_last = lambda xt: jnp.transpose(xt.reshape(depth, b, N_HEADS, HEAD_DIM, s), (0, 1, 4, 2, 3))
    return (hp, hs.reshape(db, t_new, d), heads_last(kt_all), heads_last(vt_all), jnp.swapaxes(lft_all, 2, 3),
            jnp.stack(pool_p), jnp.stack(k_s), jnp.stack(v_s), jnp.stack(lf_s_all), jnp.stack(pool_s))
```
